```python
import math
import jax, jax.numpy as jnp
from jax import lax
import numpy as np

D_MODEL = 2048
BATCH = 2
SEQ = 4096
DEPTH = 2
DEC_BATCH = 8
DEC_SEQ = 4
PAST_LEN = 16384
PAGE_SIZE = 128

N_A_LAYERS = DEPTH // 2
N_B_LAYERS = DEPTH - N_A_LAYERS
RET_HEADS = 8
RET_DK = D_MODEL // RET_HEADS
RET_DV = 2 * RET_DK
RET_CHUNK = 128
ROPE_BASE = 10000.0
SB_HEADS = 16
SB_DH = D_MODEL // SB_HEADS
SB_BLOCK = 128
SB_BIAS_HI = -4.0
SB_BIAS_LO = -8.0
D_FF = 4 * D_MODEL
LN_EPS = 1e-5
GN_EPS = 1e-6
ALPHA = (2.0 * DEPTH) ** 0.25
BETA = (8.0 * DEPTH) ** -0.25
POOL_FACTOR = 1.25

kernel_name = "yoco_retention_stickbreaking_decoder_step"


def layer_norm(x, g, b):
    xf = x.astype(jnp.float32)
    mu = jnp.mean(xf, axis=-1, keepdims=True)
    var = jnp.mean(jnp.square(xf - mu), axis=-1, keepdims=True)
    return ((xf - mu) * lax.rsqrt(var + LN_EPS) * g.astype(jnp.float32) + b.astype(jnp.float32)).astype(x.dtype)


def post_norm(x, sub, g, b):
    return layer_norm(ALPHA * x + sub, g, b)


def rotary(x, pos):
    half = x.shape[-1] // 2
    inv = ROPE_BASE ** (-jnp.arange(half, dtype=jnp.float32) / half)
    ang = pos.astype(jnp.float32)[:, None] * inv[None, :]
    cos = jnp.cos(ang)[None, :, None, :]
    sin = jnp.sin(ang)[None, :, None, :]
    xf = x.astype(jnp.float32)
    x1, x2 = xf[..., :half], xf[..., half:]
    return jnp.concatenate([x1 * cos - x2 * sin, x1 * sin + x2 * cos], axis=-1).astype(x.dtype)


def retention_scan(q, k, v, state0):
    B, T, H, DK = q.shape
    DV = v.shape[-1]
    C = RET_CHUNK if T % RET_CHUNK == 0 else T
    n = T // C
    log_g = jnp.log1p(-jnp.power(2.0, -5.0 - jnp.arange(H, dtype=jnp.float32)))
    idx = jnp.arange(C, dtype=jnp.float32)
    diff = idx[:, None] - idx[None, :]
    decay_in = jnp.where(diff[None] >= 0, jnp.exp(log_g[:, None, None] * jnp.maximum(diff, 0.0)[None]), 0.0)
    decay_q = jnp.exp(log_g[:, None] * (idx + 1.0)[None, :])
    decay_k = jnp.exp(log_g[:, None] * (C - 1.0 - idx)[None, :])
    decay_c = jnp.exp(log_g * C)

    def to_chunks(a):
        return a.astype(jnp.float32).reshape(B, n, C, H, a.shape[-1]).transpose(1, 0, 3, 2, 4)

    def step(S, inp):
        qc, kc, vc = inp
        scores = jnp.einsum('bhid,bhjd->bhij', qc, kc) * decay_in[None]
        inner = jnp.einsum('bhij,bhjv->bhiv', scores, vc)
        cross = jnp.einsum('bhid,bhdv->bhiv', qc, S) * decay_q[None, :, :, None]
        S_new = decay_c[None, :, None, None] * S + jnp.einsum('bhjd,bhjv->bhdv', kc * decay_k[None, :, :, None], vc)
        return S_new, inner + cross

    S_final, out = lax.scan(step, state0, (to_chunks(q), to_chunks(k), to_chunks(v)))
    out = out.transpose(1, 0, 3, 2, 4).reshape(B, T, H, DV)
    return out, S_final


def retention_mixer(x, pos, state0, w_in, w_out):
    B, T, _ = x.shape
    HK = RET_HEADS * RET_DK
    HV = RET_HEADS * RET_DV
    proj = jnp.einsum('btd,de->bte', x, w_in)
    q, k, v, g = jnp.split(proj, [HK, 2 * HK, 2 * HK + HV], axis=-1)
    q = rotary(q.reshape(B, T, RET_HEADS, RET_DK), pos) * (RET_DK ** -0.5)
    k = rotary(k.reshape(B, T, RET_HEADS, RET_DK), pos)
    v = v.reshape(B, T, RET_HEADS, RET_DV)
    o, S = retention_scan(q, k, v, state0)
    mu = jnp.mean(o, axis=-1, keepdims=True)
    var = jnp.mean(jnp.square(o - mu), axis=-1, keepdims=True)
    o = ((o - mu) * lax.rsqrt(var + GN_EPS)).reshape(B, T, HV)
    o = jax.nn.silu(g.astype(jnp.float32)) * o
    return jnp.einsum('bte,ed->btd', o.astype(x.dtype), w_out), S


def stick_breaking_attend(q, k, v, bias, q_pos0):
    B, T, H, Dh = q.shape
    L = k.shape[1]
    blk = SB_BLOCK if T % SB_BLOCK == 0 else T
    nb = T // blk
    qb = q.astype(jnp.float32).reshape(B, nb, blk, H, Dh).transpose(1, 0, 2, 3, 4)
    kf = k.astype(jnp.float32)
    vf = v.astype(jnp.float32)
    bf = bias.astype(jnp.float32)[None, :, None, None]
    key_pos = jnp.arange(L, dtype=jnp.int32)
    scale = Dh ** -0.5

    def one_block(args):
        qblk, b_idx = args
        q_pos = q_pos0 + b_idx * blk + jnp.arange(blk, dtype=jnp.int32)
        z = jnp.einsum('bqhd,bkhd->bhqk', qblk, kf) * scale + bf
        visible = (key_pos[None, :] < q_pos[:, None])[None, None]
        log_rem = jnp.where(visible, jax.nn.log_sigmoid(-z), 0.0)
        after = lax.cumsum(log_rem, axis=3, reverse=True) - log_rem
        w = jnp.where(visible, jnp.exp(jax.nn.log_sigmoid(z) + after), 0.0)
        return jnp.einsum('bhqk,bkhd->bqhd', w, vf)

    out = lax.map(one_block, (qb, jnp.arange(nb, dtype=jnp.int32)))
    return out.transpose(1, 0, 2, 3, 4).reshape(B, T, H, Dh).astype(q.dtype)


def stick_breaking_mixer(x, k_all, v_all, q_pos0, w_q, w_o, bias):
    B, T, _ = x.shape
    q = jnp.einsum('btd,de->bte', x, w_q).reshape(B, T, SB_HEADS, SB_DH)
    o = stick_breaking_attend(q, k_all, v_all, bias, q_pos0)
    return jnp.einsum('bte,ed->btd', o.reshape(B, T, SB_HEADS * SB_DH), w_o)


def sq_relu_mlp(x, w1, w2):
    h = jnp.einsum('btd,df->btf', x, w1)
    return jnp.einsum('btf,fd->btd', jnp.square(jax.nn.relu(h)), w2)


def trunk(x, pos0, ret_states, past_k, past_v, w_ret_in, w_ret_out, w_kv, w_sb_q, w_sb_o, sb_bias,
          w_ff1, w_ff2, ln_g, ln_b):
    B, T, _ = x.shape
    pos = pos0 + jnp.arange(T, dtype=jnp.int32)
    new_ret = []
    k_new = v_new = k_all = v_all = None
    for layer in range(DEPTH):
        if layer < N_A_LAYERS:
            mix, S = retention_mixer(x, pos, ret_states[layer], w_ret_in[layer], w_ret_out[layer])
            new_ret.append(S)
        else:
            j = layer - N_A_LAYERS
            mix = stick_breaking_mixer(x, k_all, v_all, pos0, w_sb_q[j], w_sb_o[j], sb_bias[j])
        x = post_norm(x, mix, ln_g[layer, 0], ln_b[layer, 0])
        x = post_norm(x, sq_relu_mlp(x, w_ff1[layer], w_ff2[layer]), ln_g[layer, 1], ln_b[layer, 1])
        if layer == N_A_LAYERS - 1:
            kv = jnp.einsum('btd,de->bte', x, w_kv)
            k_new = kv[..., :D_MODEL].reshape(B, T, SB_HEADS, SB_DH)
            v_new = kv[..., D_MODEL:].reshape(B, T, SB_HEADS, SB_DH)
            if past_k is None:
                k_all, v_all = k_new, v_new
            else:
                k_all = jnp.concatenate([past_k.astype(k_new.dtype), k_new], axis=1)
                v_all = jnp.concatenate([past_v.astype(v_new.dtype), v_new], axis=1)
    return x, jnp.stack(new_ret), k_new, v_new


def setup_inputs(seed: int = 0) -> dict:
    key = jax.random.key(seed)
    ks = jax.random.split(key, 20)
    n_pages = PAST_LEN // PAGE_SIZE
    n_used = DEC_BATCH * n_pages
    n_pool = int(math.ceil(POOL_FACTOR * n_used))
    HK = RET_HEADS * RET_DK
    HV = RET_HEADS * RET_DV
    s_d = D_MODEL ** -0.5
    w_ret_in = jnp.concatenate([
        jax.random.normal(ks[0], (N_A_LAYERS, D_MODEL, 2 * HK), jnp.float32) * s_d,
        jax.random.normal(ks[1], (N_A_LAYERS, D_MODEL, HV), jnp.float32) * (s_d * BETA),
        jax.random.normal(ks[2], (N_A_LAYERS, D_MODEL, HV), jnp.float32) * s_d,
    ], axis=-1)
    w_ret_out = jax.random.normal(ks[3], (N_A_LAYERS, HV, D_MODEL), jnp.float32) * (HV ** -0.5 * BETA)
    w_kv = jnp.concatenate([
        jax.random.normal(ks[4], (D_MODEL, D_MODEL), jnp.float32) * s_d,
        jax.random.normal(ks[5], (D_MODEL, D_MODEL), jnp.float32) * (s_d * BETA),
    ], axis=-1)
    w_sb_q = jax.random.normal(ks[6], (N_B_LAYERS, D_MODEL, D_MODEL), jnp.float32) * s_d
    w_sb_o = jax.random.normal(ks[7], (N_B_LAYERS, D_MODEL, D_MODEL), jnp.float32) * (s_d * BETA)
    sb_bias = (jnp.linspace(SB_BIAS_HI, SB_BIAS_LO, SB_HEADS, dtype=jnp.float32)[None, :]
               + 0.05 * jax.random.normal(ks[18], (N_B_LAYERS, SB_HEADS), jnp.float32))
    w_ff1 = jax.random.normal(ks[8], (DEPTH, D_MODEL, D_FF), jnp.float32) * (s_d * BETA)
    w_ff2 = jax.random.normal(ks[9], (DEPTH, D_FF, D_MODEL), jnp.float32) * (D_FF ** -0.5 * BETA)
    ln_g = 1.0 + 0.02 * jax.random.normal(ks[10], (DEPTH, 2, D_MODEL), jnp.float32)
    ln_b = 0.02 * jax.random.normal(ks[11], (DEPTH, 2, D_MODEL), jnp.float32)
    x_prompt = jax.random.normal(ks[12], (BATCH, SEQ, D_MODEL), jnp.float32)
    x_sample = jax.random.normal(ks[13], (DEC_BATCH, DEC_SEQ, D_MODEL), jnp.float32)
    state_ret = jax.random.normal(ks[14], (N_A_LAYERS, DEC_BATCH, RET_HEADS, RET_DK, RET_DV), jnp.float32)
    cache_k = jax.random.normal(ks[15], (n_pool, PAGE_SIZE, SB_HEADS, SB_DH), jnp.float32)
    cache_v = jax.random.normal(ks[16], (n_pool, PAGE_SIZE, SB_HEADS, SB_DH), jnp.float32) * BETA
    page_table = jax.random.permutation(ks[17], n_pool)[:n_used].reshape(DEC_BATCH, n_pages).astype(jnp.int32)
    return {"x_prompt": x_prompt, "x_sample": x_sample, "state_ret": state_ret,
            "cache_k": cache_k, "cache_v": cache_v, "page_table": page_table,
            "w_ret_in": w_ret_in, "w_ret_out": w_ret_out, "w_kv": w_kv,
            "w_sb_q": w_sb_q, "w_sb_o": w_sb_o, "sb_bias": sb_bias, "w_ff1": w_ff1, "w_ff2": w_ff2,
            "ln_g": ln_g, "ln_b": ln_b}


def reference(x_prompt, x_sample, state_ret, cache_k, cache_v, page_table,
              w_ret_in, w_ret_out, w_kv, w_sb_q, w_sb_o, sb_bias, w_ff1, w_ff2, ln_g, ln_b):
    zero_ret = jnp.zeros((N_A_LAYERS, x_prompt.shape[0], RET_HEADS, RET_DK, RET_DV), jnp.float32)
    y_prompt, ret_prompt, k_prompt, v_prompt = trunk(
        x_prompt, 0, zero_ret, None, None, w_ret_in, w_ret_out, w_kv, w_sb_q, w_sb_o, sb_bias,
        w_ff1, w_ff2, ln_g, ln_b)
    db, n_pages = page_table.shape
    past_len = n_pages * PAGE_SIZE
    past_k = cache_k[page_table].reshape(db, past_len, SB_HEADS, SB_DH)
    past_v = cache_v[page_table].reshape(db, past_len, SB_HEADS, SB_DH)
    y_sample, ret_sample, k_sample, v_sample = trunk(
        x_sample, past_len, state_ret.astype(jnp.float32), past_k, past_v, w_ret_in, w_ret_out, w_kv,
        w_sb_q, w_sb_o, sb_bias, w_ff1, w_ff2, ln_g, ln_b)
    ret_prompt = ret_prompt.astype(state_ret.dtype)
    ret_sample = ret_sample.astype(state_ret.dtype)
    return (y_prompt, y_sample, ret_prompt, k_prompt, v_prompt, ret_sample, k_sample, v_sample)
```

```python
import functools

import jax
import jax.numpy as jnp
from jax import lax
from jax.experimental import pallas as pl
from jax.experimental.pallas import tpu as pltpu

F32 = jnp.float32
BF16 = jnp.bfloat16

RET_HEADS = 8
RET_CHUNK = 128
ROPE_BASE = 10000.0
SB_HEADS = 16
LN_EPS = 1e-5
GN_EPS = 1e-6
DEPTH = 2
ALPHA = (2.0 * DEPTH) ** 0.25

V7X_VMEM_BYTES = 64 * 1024 * 1024
VMEM_LIMIT_BYTES = V7X_VMEM_BYTES - 12 * 1024 * 1024
LANES = 128


def _params(semantics):
    return pltpu.CompilerParams(dimension_semantics=semantics, vmem_limit_bytes=VMEM_LIMIT_BYTES)


def _row_tile(m, want):
    return want if m % want == 0 else m


def _proj_kernel(x_ref, w_ref, *refs, n_out, cast_x):
    outs = refs[:n_out]
    if cast_x:
        xb_ref = refs[n_out]

        @pl.when(pl.program_id(1) == 0)
        def _():
            xb_ref[...] = x_ref[...].astype(BF16)

        xb = xb_ref[...]
    else:
        xb = x_ref[...]
    acc = jnp.dot(xb, w_ref[...], preferred_element_type=F32)
    for o in outs:
        o[...] = acc.astype(o.dtype)


def _proj(x, w, out_dtypes, *, tm=1024, tn=512):
    m, k = x.shape
    n = w.shape[1]
    tm = _row_tile(m, tm)
    tn = _row_tile(n, tn)
    cast_x = x.dtype != BF16
    scratch = [pltpu.VMEM((tm, k), BF16)] if cast_x else []
    outs = pl.pallas_call(
        functools.partial(_proj_kernel, n_out=len(out_dtypes), cast_x=cast_x),
        grid=(m // tm, n // tn),
        in_specs=[pl.BlockSpec((tm, k), lambda i, j: (i, 0)),
                  pl.BlockSpec((k, tn), lambda i, j: (0, j))],
        out_specs=[pl.BlockSpec((tm, tn), lambda i, j: (i, j)) for _ in out_dtypes],
        out_shape=[jax.ShapeDtypeStruct((m, n), dt) for dt in out_dtypes],
        scratch_shapes=scratch,
        compiler_params=_params(("parallel", "arbitrary")),
        name="proj",
    )(x, w)
    return outs


def _post_norm_store(res, sub, g_ref, b_ref, of_ref, ob_ref):
    y = ALPHA * res + sub
    mu = jnp.mean(y, axis=-1, keepdims=True)
    yc = y - mu
    var = jnp.mean(yc * yc, axis=-1, keepdims=True)
    out = yc * lax.rsqrt(var + LN_EPS) * g_ref[...] + b_ref[...]
    of_ref[...] = out
    if ob_ref is not None:
        ob_ref[...] = out.astype(BF16)


def _accumulate(acc_ref, part, step):
    @pl.when(step == 0)
    def _():
        acc_ref[...] = part

    @pl.when(step > 0)
    def _():
        acc_ref[...] += part


def _proj_norm_kernel(x_ref, w_ref, res_ref, g_ref, b_ref, of_ref, ob_ref, acc_ref, *, nk):
    k = pl.program_id(1)
    _accumulate(acc_ref, jnp.dot(x_ref[...], w_ref[...], preferred_element_type=F32), k)

    @pl.when(k == nk - 1)
    def _():
        _post_norm_store(res_ref[...], acc_ref[...], g_ref, b_ref, of_ref, ob_ref)


def _proj_norm(x, w, res, g, b, *, tm=512, tk=512):
    m, kdim = x.shape
    n = w.shape[1]
    tm = _row_tile(m, tm)
    nk = kdim // tk
    row = lambda i, k: (i, 0)
    return pl.pallas_call(
        functools.partial(_proj_norm_kernel, nk=nk),
        grid=(m // tm, nk),
        in_specs=[pl.BlockSpec((tm, tk), lambda i, k: (i, k)),
                  pl.BlockSpec((tk, n), lambda i, k: (k, 0)),
                  pl.BlockSpec((tm, n), row),
                  pl.BlockSpec((1, n), lambda i, k: (0, 0)),
                  pl.BlockSpec((1, n), lambda i, k: (0, 0))],
        out_specs=[pl.BlockSpec((tm, n), row), pl.BlockSpec((tm, n), row)],
        out_shape=[jax.ShapeDtypeStruct((m, n), F32), jax.ShapeDtypeStruct((m, n), BF16)],
        scratch_shapes=[pltpu.VMEM((tm, n), F32)],
        compiler_params=_params(("parallel", "arbitrary")),
        name="proj_norm",
    )(x, w, res, g, b)


def _mlp_kernel(x_ref, xb_ref, w1_ref, w2_ref, g_ref, b_ref, of_ref, ob_ref, acc_ref, *, nf):
    f = pl.program_id(1)
    h = jnp.dot(xb_ref[...], w1_ref[...], preferred_element_type=F32)
    h = jnp.square(jnp.maximum(h, 0.0)).astype(BF16)
    _accumulate(acc_ref, jnp.dot(h, w2_ref[...], preferred_element_type=F32), f)

    @pl.when(f == nf - 1)
    def _():
        _post_norm_store(x_ref[...], acc_ref[...], g_ref, b_ref, of_ref, ob_ref)


def _mlp(x, xb, w1, w2, g, b, *, tm=512, tf=512):
    m, d = x.shape
    dff = w1.shape[1]
    tm = _row_tile(m, tm)
    nf = dff // tf
    row = lambda i, f: (i, 0)
    return pl.pallas_call(
        functools.partial(_mlp_kernel, nf=nf),
        grid=(m // tm, nf),
        in_specs=[pl.BlockSpec((tm, d), row),
                  pl.BlockSpec((tm, d), row),
                  pl.BlockSpec((d, tf), lambda i, f: (0, f)),
                  pl.BlockSpec((tf, d), lambda i, f: (f, 0)),
                  pl.BlockSpec((1, d), lambda i, f: (0, 0)),
                  pl.BlockSpec((1, d), lambda i, f: (0, 0))],
        out_specs=[pl.BlockSpec((tm, d), row), pl.BlockSpec((tm, d), row)],
        out_shape=[jax.ShapeDtypeStruct((m, d), F32), jax.ShapeDtypeStruct((m, d), BF16)],
        scratch_shapes=[pltpu.VMEM((tm, d), F32)],
        compiler_params=_params(("parallel", "arbitrary")),
        name="mlp",
    )(x, xb, w1, w2, g, b)


def _rotary(x, cos, sin):
    half = x.shape[-1] // 2
    x1, x2 = x[:, :half], x[:, half:]
    return jnp.concatenate([x1 * cos - x2 * sin, x1 * sin + x2 * cos], axis=-1)


def _retention_kernel(*refs, n_chunks, has_state0, q_scale):
    if has_state0:
        s0_ref, refs = refs[0], refs[1:]
    (q_ref, k_ref, v_ref, g_ref, cos_ref, sin_ref, din_ref, dq_ref, dk_ref, dc_ref,
     o_ref, sout_ref, s_ref) = refs
    c = pl.program_id(2)

    @pl.when(c == 0)
    def _():
        if has_state0:
            s_ref[...] = s0_ref[0, 0]
        else:
            s_ref[...] = jnp.zeros_like(s_ref)

    cos, sin = cos_ref[...], sin_ref[...]
    q = (_rotary(q_ref[...], cos, sin) * q_scale).astype(BF16)
    k = _rotary(k_ref[...], cos, sin)
    v = v_ref[...].astype(BF16)
    state = s_ref[...]

    scores = lax.dot_general(q, k.astype(BF16), (((1,), (1,)), ((), ())),
                             preferred_element_type=F32) * din_ref[0]
    inner = jnp.dot(scores.astype(BF16), v, preferred_element_type=F32)
    cross = jnp.dot(q, state.astype(BF16), preferred_element_type=F32) * dq_ref[0]
    kd = (k * dk_ref[0]).astype(BF16)
    s_ref[...] = dc_ref[0] * state + lax.dot_general(kd, v, (((0,), (0,)), ((), ())),
                                                     preferred_element_type=F32)

    o = inner + cross
    mu = jnp.mean(o, axis=-1, keepdims=True)
    oc = o - mu
    var = jnp.mean(oc * oc, axis=-1, keepdims=True)
    gate = g_ref[...]
    gate = gate / (1.0 + jnp.exp(-gate))
    o_ref[...] = (gate * (oc * lax.rsqrt(var + GN_EPS))).astype(o_ref.dtype)

    @pl.when(c == n_chunks - 1)
    def _():
        sout_ref[0, 0] = s_ref[...]


def _retention_tables(chunk, n_valid):
    h = RET_HEADS
    log_g = jnp.log1p(-jnp.power(2.0, -5.0 - jnp.arange(h, dtype=F32)))
    idx = jnp.arange(chunk, dtype=F32)
    diff = idx[:, None] - idx[None, :]
    din = jnp.where(diff[None] >= 0, jnp.exp(log_g[:, None, None] * jnp.maximum(diff, 0.0)[None]), 0.0)
    dq = jnp.exp(log_g[:, None] * (idx + 1.0)[None, :])[..., None]
    dk = jnp.exp(log_g[:, None] * (n_valid - 1.0 - idx)[None, :])[..., None]
    dc = jnp.exp(log_g * n_valid).reshape(h, 1, 1)
    return din, dq, dk, dc


def _rotary_tables(pos, half):
    inv = ROPE_BASE ** (-jnp.arange(half, dtype=F32) / half)
    ang = pos.astype(F32)[:, None] * inv[None, :]
    return jnp.cos(ang), jnp.sin(ang)


def _retention(proj, state0, *, batch, seq, chunk, n_valid, pos0):
    h = RET_HEADS
    width = proj.shape[1]
    dk_dim = width // (6 * h)
    dv_dim = 2 * dk_dim
    n_chunks = seq // chunk
    cos, sin = _rotary_tables(pos0 + jnp.arange(seq, dtype=jnp.int32), dk_dim // 2)
    din, dq, dk, dc = _retention_tables(chunk, n_valid)
    has_state0 = state0 is not None

    row = lambda b, hh, c: b * n_chunks + c
    in_specs = [
        pl.BlockSpec((chunk, dk_dim), lambda b, hh, c: (row(b, hh, c), hh)),
        pl.BlockSpec((chunk, dk_dim), lambda b, hh, c: (row(b, hh, c), h + hh)),
        pl.BlockSpec((chunk, dv_dim), lambda b, hh, c: (row(b, hh, c), h + hh)),
        pl.BlockSpec((chunk, dv_dim), lambda b, hh, c: (row(b, hh, c), 2 * h + hh)),
        pl.BlockSpec((chunk, dk_dim // 2), lambda b, hh, c: (c, 0)),
        pl.BlockSpec((chunk, dk_dim // 2), lambda b, hh, c: (c, 0)),
        pl.BlockSpec((1, chunk, chunk), lambda b, hh, c: (hh, 0, 0)),
        pl.BlockSpec((1, chunk, 1), lambda b, hh, c: (hh, 0, 0)),
        pl.BlockSpec((1, chunk, 1), lambda b, hh, c: (hh, 0, 0)),
        pl.BlockSpec((1, 1, 1), lambda b, hh, c: (hh, 0, 0)),
    ]
    args = [proj, proj, proj, proj, cos, sin, din, dq, dk, dc]
    if has_state0:
        in_specs.insert(0, pl.BlockSpec((1, 1, dk_dim, dv_dim), lambda b, hh, c: (b, hh, 0, 0)))
        args.insert(0, state0)
    o, s_out = pl.pallas_call(
        functools.partial(_retention_kernel, n_chunks=n_chunks, has_state0=has_state0,
                          q_scale=dk_dim ** -0.5),
        grid=(batch, h, n_chunks),
        in_specs=in_specs,
        out_specs=[pl.BlockSpec((chunk, dv_dim), lambda b, hh, c: (row(b, hh, c), hh)),
                   pl.BlockSpec((1, 1, dk_dim, dv_dim), lambda b, hh, c: (b, hh, 0, 0))],
        out_shape=[jax.ShapeDtypeStruct((batch * seq, h * dv_dim), BF16),
                   jax.ShapeDtypeStruct((batch, h, dk_dim, dv_dim), F32)],
        scratch_shapes=[pltpu.VMEM((dk_dim, dv_dim), F32)],
        compiler_params=_params(("parallel", "parallel", "arbitrary")),
        name="retention",
    )(*args)
    return o, s_out


def _log_terms(z):
    l1p = jnp.log1p(jnp.exp(-jnp.abs(z)))
    return -jnp.maximum(z, 0.0) - l1p, jnp.minimum(z, 0.0) - l1p


def _split_bf16(x):
    hi = x.astype(BF16)
    return hi, (x - hi.astype(F32)).astype(BF16)


def _sb_prompt_kernel(q_ref, k_ref, v_ref, bias_ref, o_ref, *, tq, scale):
    qi = pl.program_id(2)
    q = q_ref[...]
    bias = bias_ref[0]
    r = lax.broadcasted_iota(jnp.int32, (tq, tq), 0)
    c = lax.broadcasted_iota(jnp.int32, (tq, tq), 1)
    suffix = jnp.where(r > c, 1.0, 0.0).astype(BF16)
    visible = c < r

    def block(kb, carry, acc, masked):
        start = pl.multiple_of(kb * tq, tq)
        kblk = k_ref[pl.ds(start, tq), :]
        vblk = v_ref[pl.ds(start, tq), :]
        z = lax.dot_general(q, kblk, (((1,), (1,)), ((), ())), preferred_element_type=F32) * scale + bias
        lr, lb = _log_terms(z)
        if masked:
            lr = jnp.where(visible, lr, 0.0)
        hi, lo = _split_bf16(lr)
        after = (jnp.dot(hi, suffix, preferred_element_type=F32)
                 + jnp.dot(lo, suffix, preferred_element_type=F32)) + carry
        w = jnp.exp(lb + after)
        if masked:
            w = jnp.where(visible, w, 0.0)
        acc = acc + jnp.dot(w.astype(BF16), vblk, preferred_element_type=F32)
        carry = carry + jnp.sum(lr, axis=-1, keepdims=True)
        return carry, acc

    carry = jnp.zeros((tq, 1), F32)
    acc = jnp.zeros((tq, q.shape[-1]), F32)
    carry, acc = block(qi, carry, acc, True)

    def body(it, state):
        return block(qi - 1 - it, state[0], state[1], False)

    carry, acc = lax.fori_loop(0, qi, body, (carry, acc))
    o_ref[...] = acc.astype(o_ref.dtype)


def _sb_prompt(q, k, v, bias, *, batch, seq, tq=256):
    h = SB_HEADS
    dh = q.shape[1] // h
    nq = seq // tq
    return pl.pallas_call(
        functools.partial(_sb_prompt_kernel, tq=tq, scale=dh ** -0.5),
        grid=(batch, h, nq),
        in_specs=[pl.BlockSpec((tq, dh), lambda b, hh, i: (b * nq + i, hh)),
                  pl.BlockSpec((seq, dh), lambda b, hh, i: (b, hh)),
                  pl.BlockSpec((seq, dh), lambda b, hh, i: (b, hh)),
                  pl.BlockSpec((1, 1, 1), lambda b, hh, i: (hh, 0, 0))],
        out_specs=pl.BlockSpec((tq, dh), lambda b, hh, i: (b * nq + i, hh)),
        out_shape=jax.ShapeDtypeStruct(q.shape, BF16),
        compiler_params=_params(("parallel", "parallel", "parallel")),
        name="sb_prompt",
    )(q, k, v, bias.reshape(h, 1, 1))


def _sb_decode_kernel(pt_ref, qbd_ref, bias_ref, kn_ref, vn_ref, *refs, pages_per_step, n_steps, page, scale, t_new):
    del pt_ref
    k_refs = refs[:pages_per_step]
    v_refs = refs[pages_per_step:2 * pages_per_step]
    o_ref, acc_ref, carry_ref = refs[2 * pages_per_step:]
    p = pl.program_id(1)
    qbd = qbd_ref[0]
    bias = bias_ref[...]
    ncol = qbd.shape[1]
    r = lax.broadcasted_iota(jnp.int32, (page, page), 0)
    c = lax.broadcasted_iota(jnp.int32, (page, page), 1)
    suffix_t = jnp.where(c > r, 1.0, 0.0).astype(BF16)

    def page_update(kp, vp, masked):
        zt = jnp.dot(kp.astype(BF16), qbd, preferred_element_type=F32) * scale + bias
        lr, lb = _log_terms(zt)
        if masked:
            key_j = lax.broadcasted_iota(jnp.int32, (page, ncol), 0)
            qry_t = lax.broadcasted_iota(jnp.int32, (page, ncol), 1) % t_new
            vis = key_j < qry_t
            lr = jnp.where(vis, lr, 0.0)
        hi, lo = _split_bf16(lr)
        after = (jnp.dot(suffix_t, hi, preferred_element_type=F32)
                 + jnp.dot(suffix_t, lo, preferred_element_type=F32)) + carry_ref[...]
        w = jnp.exp(lb + after)
        if masked:
            w = jnp.where(vis, w, 0.0)
        acc_ref[...] += lax.dot_general(w.astype(BF16), vp.astype(BF16), (((0,), (0,)), ((), ())),
                                        preferred_element_type=F32)
        carry_ref[...] += jnp.sum(lr, axis=0, keepdims=True)

    @pl.when(p == 0)
    def _():
        acc_ref[...] = jnp.zeros_like(acc_ref)
        carry_ref[...] = jnp.zeros_like(carry_ref)
        page_update(kn_ref[0], vn_ref[0], True)

    for g in range(pages_per_step):
        page_update(k_refs[g][0], v_refs[g][0], False)

    @pl.when(p == n_steps - 1)
    def _():
        o_ref[0] = acc_ref[...]


def _sb_decode(q, k_new, v_new, cache_k, cache_v, page_table, bias, *, batch, t_new, pages_per_step=4):
    h = SB_HEADS
    n_pool, page, _, dh = cache_k.shape
    hd = h * dh
    n_pages = page_table.shape[1]
    n_steps = n_pages // pages_per_step
    ncol = h * t_new
    ck = cache_k.reshape(n_pool, page, hd)
    cv = cache_v.reshape(n_pool, page, hd)
    q4 = q.reshape(batch, t_new, h, dh)
    qbd = jnp.einsum("bthd,hg->bhdgt", q4, jnp.eye(h, dtype=q.dtype)).reshape(batch, hd, ncol).astype(BF16)
    bias_cols = jnp.repeat(bias.astype(F32), t_new)[None, :]
    pad = lambda a: jnp.pad(a.reshape(batch, t_new, hd), ((0, 0), (0, page - t_new), (0, 0)))
    kn, vn = pad(k_new), pad(v_new)

    def page_map(g):
        return lambda b, p, pt: (pt[b, n_pages - 1 - (p * pages_per_step + g)], 0, 0)

    page_specs = [pl.BlockSpec((1, page, hd), page_map(g)) for g in range(pages_per_step)]
    per_batch = lambda b, p, pt: (b, 0, 0)
    acc = pl.pallas_call(
        functools.partial(_sb_decode_kernel, pages_per_step=pages_per_step, n_steps=n_steps, page=page,
                          scale=dh ** -0.5, t_new=t_new),
        grid_spec=pltpu.PrefetchScalarGridSpec(
            num_scalar_prefetch=1,
            grid=(batch, n_steps),
            in_specs=[pl.BlockSpec((1, hd, ncol), per_batch),
                      pl.BlockSpec((1, ncol), lambda b, p, pt: (0, 0)),
                      pl.BlockSpec((1, page, hd), per_batch),
                      pl.BlockSpec((1, page, hd), per_batch)] + page_specs + page_specs,
            out_specs=pl.BlockSpec((1, ncol, hd), per_batch),
            scratch_shapes=[pltpu.VMEM((ncol, hd), F32), pltpu.VMEM((1, ncol), F32)],
        ),
        out_shape=jax.ShapeDtypeStruct((batch, ncol, hd), F32),
        compiler_params=_params(("parallel", "arbitrary")),
        name="sb_decode",
    )(page_table, qbd, bias_cols, kn, vn, *([ck] * pages_per_step), *([cv] * pages_per_step))
    acc = acc.reshape(batch, h, t_new, h, dh)
    idx = jnp.arange(h)
    o = acc[:, idx, :, idx, :]
    return jnp.transpose(o, (1, 2, 0, 3)).reshape(batch * t_new, hd).astype(BF16)


def _trunk(x, weights, *, batch, seq, pos0, state0, decode):
    (w_ret_in, w_ret_out, w_kv, w_sb_q, w_sb_o, sb_bias, w_ff1, w_ff2, ln_g, ln_b) = weights
    m, d = x.shape
    g = lambda layer, i: ln_g[layer, i][None, :]
    b = lambda layer, i: ln_b[layer, i][None, :]

    proj = _proj(x, w_ret_in[0], [F32])[0]
    if seq % RET_CHUNK == 0:
        o, s_new = _retention(proj, state0, batch=batch, seq=seq, chunk=RET_CHUNK, n_valid=RET_CHUNK, pos0=pos0)
    else:
        chunk = 8
        padded = jnp.pad(proj.reshape(batch, seq, -1), ((0, 0), (0, chunk - seq), (0, 0)))
        o, s_new = _retention(padded.reshape(batch * chunk, -1), state0, batch=batch, seq=chunk, chunk=chunk,
                              n_valid=seq, pos0=pos0)
        o = o.reshape(batch, chunk, -1)[:, :seq].reshape(m, -1)
    x, xb = _proj_norm(o, w_ret_out[0], x, g(0, 0), b(0, 0))
    x, xb = _mlp(x, xb, w_ff1[0], w_ff2[0], g(0, 1), b(0, 1))

    k_new, kb = _proj(xb, w_kv[0], [F32, BF16])
    v_new, vb = _proj(xb, w_kv[1], [F32, BF16])
    if decode is None:
        q = _proj(xb, w_sb_q[0], [BF16])[0]
        o = _sb_prompt(q, kb, vb, sb_bias[0], batch=batch, seq=seq)
    else:
        cache_k, cache_v, page_table = decode
        q = _proj(xb, w_sb_q[0], [F32])[0]
        o = _sb_decode(q, k_new, v_new, cache_k, cache_v, page_table, sb_bias[0], batch=batch, t_new=seq)
    x, xb = _proj_norm(o, w_sb_o[0], x, g(1, 0), b(1, 0))
    y, _ = _mlp(x, xb, w_ff1[1], w_ff2[1], g(1, 1), b(1, 1))
    return y, s_new, k_new, v_new


def kernel(x_prompt, x_sample, state_ret, cache_k, cache_v, page_table, w_ret_in, w_ret_out, w_kv, w_sb_q,
           w_sb_o, sb_bias, w_ff1, w_ff2, ln_g, ln_b):
    bp, tp, d = x_prompt.shape
    bs, ts, _ = x_sample.shape
    dh = d // SB_HEADS
    n_pages = page_table.shape[1]
    page = cache_k.shape[1]
    cast = lambda w: w.astype(BF16)
    weights = (cast(w_ret_in), cast(w_ret_out), (cast(w_kv[:, :d]), cast(w_kv[:, d:])), cast(w_sb_q),
               cast(w_sb_o), sb_bias,
               cast(w_ff1), cast(w_ff2), ln_g.astype(F32), ln_b.astype(F32))

    y_p, s_p, k_p, v_p = _trunk(x_prompt.reshape(bp * tp, d), weights, batch=bp, seq=tp, pos0=0,
                                state0=None, decode=None)
    y_s, s_s, k_s, v_s = _trunk(x_sample.reshape(bs * ts, d), weights, batch=bs, seq=ts, pos0=n_pages * page,
                                state0=state_ret[0].astype(F32), decode=(cache_k, cache_v, page_table))
    heads = lambda a, bb, tt: a.reshape(bb, tt, SB_HEADS, dh)
    return (y_p.reshape(bp, tp, d), y_s.reshape(bs, ts, d),
            s_p[None].astype(state_ret.dtype), heads(k_p, bp, tp), heads(v_p, bp, tp),
            s_s[None].astype(state_ret.dtype), heads(k_s, bs, ts), heads(v_s, bs, ts))
```

```python
import functools

import jax
import jax.numpy as jnp
from jax import lax
from jax.experimental import pallas as pl
from jax.experimental.pallas import tpu as pltpu

F32 = jnp.float32
BF16 = jnp.bfloat16

RET_HEADS = 8
RET_CHUNK = 128
ROPE_BASE = 10000.0
SB_HEADS = 16
LN_EPS = 1e-5
GN_EPS = 1e-6
DEPTH = 2
ALPHA = (2.0 * DEPTH) ** 0.25

V7X_VMEM_BYTES = 64 * 1024 * 1024
VMEM_LIMIT_BYTES = V7X_VMEM_BYTES - 12 * 1024 * 1024
LANES = 128
SUBLANES_F32 = 8
LOG2_E = 1.4426950408889634


def _params(semantics):
    return pltpu.CompilerParams(dimension_semantics=semantics, vmem_limit_bytes=VMEM_LIMIT_BYTES)


def _row_tile(m, want):
    return want if m % want == 0 else m


def _proj_kernel(x_ref, w_ref, *refs, n_out, cast_x):
    outs = refs[:n_out]
    if cast_x:
        xb_ref = refs[n_out]

        @pl.when(pl.program_id(1) == 0)
        def _():
            xb_ref[...] = x_ref[...].astype(BF16)

        xb = xb_ref[...]
    else:
        xb = x_ref[...]
    acc = jnp.dot(xb, w_ref[...], preferred_element_type=F32)
    for o in outs:
        o[...] = acc.astype(o.dtype)


def _proj(x, w, out_dtypes, *, tm=1024, tn=512):
    m, k = x.shape
    n = w.shape[1]
    tm = _row_tile(m, tm)
    tn = _row_tile(n, tn)
    cast_x = x.dtype != BF16
    scratch = [pltpu.VMEM((tm, k), BF16)] if cast_x else []
    outs = pl.pallas_call(
        functools.partial(_proj_kernel, n_out=len(out_dtypes), cast_x=cast_x),
        grid=(m // tm, n // tn),
        in_specs=[pl.BlockSpec((tm, k), lambda i, j: (i, 0)),
                  pl.BlockSpec((k, tn), lambda i, j: (0, j))],
        out_specs=[pl.BlockSpec((tm, tn), lambda i, j: (i, j)) for _ in out_dtypes],
        out_shape=[jax.ShapeDtypeStruct((m, n), dt) for dt in out_dtypes],
        scratch_shapes=scratch,
        compiler_params=_params(("parallel", "arbitrary")),
        name="proj",
    )(x, w)
    return outs


def _post_norm_store(res, sub, g_ref, b_ref, of_ref, ob_ref):
    y = ALPHA * res + sub
    mu = jnp.mean(y, axis=-1, keepdims=True)
    yc = y - mu
    var = jnp.mean(yc * yc, axis=-1, keepdims=True)
    out = yc * lax.rsqrt(var + LN_EPS) * g_ref[...] + b_ref[...]
    of_ref[...] = out
    ob_ref[...] = out.astype(BF16)


ACC_COLS = 512


def _accumulate_dot(acc_ref, lhs, w_ref, step):
    @pl.when(step == 0)
    def _():
        acc_ref[...] = jnp.zeros_like(acc_ref)

    n = acc_ref.shape[1]
    tn = ACC_COLS if n % ACC_COLS == 0 else n
    for j in range(n // tn):
        cols = slice(j * tn, (j + 1) * tn)
        acc_ref[:, cols] += jnp.dot(lhs, w_ref[:, cols], preferred_element_type=F32)


def _proj_norm_kernel(x_ref, w_ref, res_ref, g_ref, b_ref, of_ref, ob_ref, *scratch, nk):
    if nk == 1:
        sub = jnp.dot(x_ref[...], w_ref[...], preferred_element_type=F32)
        _post_norm_store(res_ref[...], sub, g_ref, b_ref, of_ref, ob_ref)
        return
    acc_ref, = scratch
    k = pl.program_id(1)
    _accumulate_dot(acc_ref, x_ref[...], w_ref, k)

    @pl.when(k == nk - 1)
    def _():
        _post_norm_store(res_ref[...], acc_ref[...], g_ref, b_ref, of_ref, ob_ref)


def _proj_norm(x, w, res, g, b, *, tm=512, tk=2048):
    m, kdim = x.shape
    n = w.shape[1]
    tm = _row_tile(m, tm)
    nk = kdim // tk
    row = lambda i, k: (i, 0)
    return pl.pallas_call(
        functools.partial(_proj_norm_kernel, nk=nk),
        grid=(m // tm, nk),
        in_specs=[pl.BlockSpec((tm, tk), lambda i, k: (i, k)),
                  pl.BlockSpec((tk, n), lambda i, k: (k, 0)),
                  pl.BlockSpec((tm, n), row),
                  pl.BlockSpec((1, n), lambda i, k: (0, 0)),
                  pl.BlockSpec((1, n), lambda i, k: (0, 0))],
        out_specs=[pl.BlockSpec((tm, n), row), pl.BlockSpec((tm, n), row)],
        out_shape=[jax.ShapeDtypeStruct((m, n), F32), jax.ShapeDtypeStruct((m, n), BF16)],
        scratch_shapes=[pltpu.VMEM((tm, n), F32)] if nk > 1 else [],
        compiler_params=_params(("parallel", "arbitrary")),
        name="proj_norm",
    )(x, w, res, g, b)


def _mlp_kernel(x_ref, xb_ref, w1_ref, w2_ref, g_ref, b_ref, of_ref, ob_ref, acc_ref, *, nf):
    f = pl.program_id(1)
    h = jnp.dot(xb_ref[...], w1_ref[...], preferred_element_type=F32)
    h = jnp.square(jnp.maximum(h, 0.0)).astype(BF16)
    _accumulate_dot(acc_ref, h, w2_ref, f)

    @pl.when(f == nf - 1)
    def _():
        _post_norm_store(x_ref[...], acc_ref[...], g_ref, b_ref, of_ref, ob_ref)


def _mlp(x, xb, w1, w2, g, b, *, tm=512, tf=1024):
    m, d = x.shape
    dff = w1.shape[1]
    tm = _row_tile(m, tm)
    nf = dff // tf
    row = lambda i, f: (i, 0)
    return pl.pallas_call(
        functools.partial(_mlp_kernel, nf=nf),
        grid=(m // tm, nf),
        in_specs=[pl.BlockSpec((tm, d), row),
                  pl.BlockSpec((tm, d), row),
                  pl.BlockSpec((d, tf), lambda i, f: (0, f)),
                  pl.BlockSpec((tf, d), lambda i, f: (f, 0)),
                  pl.BlockSpec((1, d), lambda i, f: (0, 0)),
                  pl.BlockSpec((1, d), lambda i, f: (0, 0))],
        out_specs=[pl.BlockSpec((tm, d), row), pl.BlockSpec((tm, d), row)],
        out_shape=[jax.ShapeDtypeStruct((m, d), F32), jax.ShapeDtypeStruct((m, d), BF16)],
        scratch_shapes=[pltpu.VMEM((tm, d), F32)],
        compiler_params=_params(("parallel", "arbitrary")),
        name="mlp",
    )(x, xb, w1, w2, g, b)


RET_HEADS_PER_STEP = 4


def _rotary(x, cos, sin):
    half = x.shape[-1] // 2
    x1, x2 = x[:, :half], x[:, half:]
    return jnp.concatenate([x1 * cos - x2 * sin, x1 * sin + x2 * cos], axis=-1)


def _retention_kernel(*refs, n_chunks, has_state0, q_scale, hb, dk_dim, dv_dim):
    if has_state0:
        s0_ref, refs = refs[0], refs[1:]
    (q_ref, k_ref, v_ref, g_ref, cos_ref, sin_ref, din_ref, dq_ref, dk_ref, dc_ref,
     o_ref, sout_ref, s_ref) = refs
    c = pl.program_id(2)

    @pl.when(c == 0)
    def _():
        if has_state0:
            s_ref[...] = s0_ref[0]
        else:
            s_ref[...] = jnp.zeros_like(s_ref)

    cos, sin = cos_ref[...], sin_ref[...]
    for i in range(hb):
        kcols = slice(i * dk_dim, (i + 1) * dk_dim)
        vcols = slice(i * dv_dim, (i + 1) * dv_dim)
        q = (_rotary(q_ref[:, kcols], cos, sin) * q_scale).astype(BF16)
        k = _rotary(k_ref[:, kcols], cos, sin)
        v = v_ref[:, vcols].astype(BF16)
        state = s_ref[i]

        scores = lax.dot_general(q, k.astype(BF16), (((1,), (1,)), ((), ())),
                                 preferred_element_type=F32) * din_ref[i]
        inner = jnp.dot(scores.astype(BF16), v, preferred_element_type=F32)
        cross = jnp.dot(q, state.astype(BF16), preferred_element_type=F32) * dq_ref[i]
        kd = (k * dk_ref[i]).astype(BF16)
        s_ref[i] = dc_ref[i] * state + lax.dot_general(kd, v, (((0,), (0,)), ((), ())),
                                                       preferred_element_type=F32)

        o = inner + cross
        mu = jnp.mean(o, axis=-1, keepdims=True)
        oc = o - mu
        var = jnp.mean(oc * oc, axis=-1, keepdims=True)
        gate = g_ref[:, vcols]
        gate = gate / (1.0 + jnp.exp(-gate))
        o_ref[:, vcols] = (gate * (oc * lax.rsqrt(var + GN_EPS))).astype(o_ref.dtype)

    @pl.when(c == n_chunks - 1)
    def _():
        sout_ref[0] = s_ref[...]


def _retention_tables(chunk, n_valid):
    h = RET_HEADS
    log_g = jnp.log1p(-jnp.power(2.0, -5.0 - jnp.arange(h, dtype=F32)))
    idx = jnp.arange(chunk, dtype=F32)
    diff = idx[:, None] - idx[None, :]
    din = jnp.where(diff[None] >= 0, jnp.exp(log_g[:, None, None] * jnp.maximum(diff, 0.0)[None]), 0.0)
    dq = jnp.exp(log_g[:, None] * (idx + 1.0)[None, :])[..., None]
    dk = jnp.exp(log_g[:, None] * (n_valid - 1.0 - idx)[None, :])[..., None]
    dc = jnp.exp(log_g * n_valid).reshape(h, 1, 1)
    return din, dq, dk, dc


def _rotary_tables(pos, half):
    inv = ROPE_BASE ** (-jnp.arange(half, dtype=F32) / half)
    ang = pos.astype(F32)[:, None] * inv[None, :]
    return jnp.cos(ang), jnp.sin(ang)


def _retention(proj, state0, *, batch, seq, chunk, n_valid, pos0):
    h = RET_HEADS
    hb = RET_HEADS_PER_STEP
    ng = h // hb
    width = proj.shape[1]
    dk_dim = width // (6 * h)
    dv_dim = 2 * dk_dim
    n_chunks = seq // chunk
    cos, sin = _rotary_tables(pos0 + jnp.arange(seq, dtype=jnp.int32), dk_dim // 2)
    din, dq, dk, dc = _retention_tables(chunk, n_valid)
    has_state0 = state0 is not None

    row = lambda b, hg, c: b * n_chunks + c
    per_group = lambda b, hg, c: (hg, 0, 0)
    in_specs = [
        pl.BlockSpec((chunk, hb * dk_dim), lambda b, hg, c: (row(b, hg, c), hg)),
        pl.BlockSpec((chunk, hb * dk_dim), lambda b, hg, c: (row(b, hg, c), ng + hg)),
        pl.BlockSpec((chunk, hb * dv_dim), lambda b, hg, c: (row(b, hg, c), ng + hg)),
        pl.BlockSpec((chunk, hb * dv_dim), lambda b, hg, c: (row(b, hg, c), 2 * ng + hg)),
        pl.BlockSpec((chunk, dk_dim // 2), lambda b, hg, c: (c, 0)),
        pl.BlockSpec((chunk, dk_dim // 2), lambda b, hg, c: (c, 0)),
        pl.BlockSpec((hb, chunk, chunk), per_group),
        pl.BlockSpec((hb, chunk, 1), per_group),
        pl.BlockSpec((hb, chunk, 1), per_group),
        pl.BlockSpec((hb, 1, 1), per_group),
    ]
    args = [proj, proj, proj, proj, cos, sin, din, dq, dk, dc]
    state_spec = pl.BlockSpec((1, hb, dk_dim, dv_dim), lambda b, hg, c: (b, hg, 0, 0))
    if has_state0:
        in_specs.insert(0, state_spec)
        args.insert(0, state0)
    o, s_out = pl.pallas_call(
        functools.partial(_retention_kernel, n_chunks=n_chunks, has_state0=has_state0,
                          q_scale=dk_dim ** -0.5, hb=hb, dk_dim=dk_dim, dv_dim=dv_dim),
        grid=(batch, ng, n_chunks),
        in_specs=in_specs,
        out_specs=[pl.BlockSpec((chunk, hb * dv_dim), lambda b, hg, c: (row(b, hg, c), hg)), state_spec],
        out_shape=[jax.ShapeDtypeStruct((batch * seq, h * dv_dim), BF16),
                   jax.ShapeDtypeStruct((batch, h, dk_dim, dv_dim), F32)],
        scratch_shapes=[pltpu.VMEM((hb, dk_dim, dv_dim), F32)],
        compiler_params=_params(("parallel", "parallel", "arbitrary")),
        name="retention",
    )(*args)
    return o, s_out


def _softplus2(z2):
    return jnp.maximum(z2, 0.0) + jnp.log2(1.0 + jnp.exp2(-jnp.abs(z2)))


_NT = (((1,), (1,)), ((), ()))


def _sb_prompt_kernel(q_ref, k_ref, v_ref, bias_ref, o_ref, *, tq, tk, scale):
    qi = pl.program_id(2)
    q = q_ref[...]
    bias2 = bias_ref[0] * LOG2_E
    scale2 = scale * LOG2_E
    nsub = tq // tk
    r = lax.broadcasted_iota(jnp.int32, (tk, tk), 0)
    c = lax.broadcasted_iota(jnp.int32, (tk, tk), 1)
    suffix = jnp.where(r > c, 1.0, 0.0).astype(BF16)
    qrow = lax.broadcasted_iota(jnp.int32, (tq, tk), 0)
    kcol = lax.broadcasted_iota(jnp.int32, (tq, tk), 1)

    def process(chunks, diagonal, state):
        carry, acc = state
        starts = [pl.multiple_of(j * tq, tq) for j in chunks]
        raws = [lax.dot_general(q, k_ref[pl.ds(st, tq), :], _NT, preferred_element_type=F32) for st in starts]
        z2s, sp2s, visibles = [], [], []
        for raw in raws:
            for s in reversed(range(nsub)):
                z2 = raw[:, s * tk:(s + 1) * tk] * scale2 + bias2
                sp2 = _softplus2(z2)
                if diagonal:
                    visibles.append(kcol + s * tk < qrow)
                    sp2 = jnp.where(visibles[-1], sp2, 0.0)
                z2s.append(z2)
                sp2s.append(sp2)
        sums = [jnp.dot(sp2.astype(BF16), suffix, preferred_element_type=F32) for sp2 in sp2s]
        ws = []
        for i, (z2, sp2, in_block) in enumerate(zip(z2s, sp2s, sums)):
            w = jnp.exp2(z2 - sp2 - (in_block + carry))
            if diagonal:
                w = jnp.where(visibles[i], w, 0.0)
            ws.append(w.astype(BF16))
            carry = carry + jnp.sum(sp2, axis=-1, keepdims=True)
        for n, st in enumerate(starts):
            w_chunk = jnp.concatenate(ws[n * nsub:(n + 1) * nsub][::-1], axis=-1)
            v_chunk = v_ref[pl.ds(st, tq), :]
            half = tq // 2
            acc = acc + jnp.concatenate(
                [jnp.dot(w_chunk[:half], v_chunk, preferred_element_type=F32),
                 jnp.dot(w_chunk[half:], v_chunk, preferred_element_type=F32)], axis=0)
        return carry, acc

    state = (jnp.zeros((tq, 1), F32), jnp.zeros((tq, q.shape[-1]), F32))
    state = process([qi], True, state)
    odd = qi % 2
    state = lax.fori_loop(0, odd, lambda it, st: process([qi - 1], False, st), state)
    first = qi - 1 - odd

    def pair(it, st):
        j = first - 2 * it
        return process([j, j - 1], False, st)

    _, acc = lax.fori_loop(0, qi // 2, pair, state)
    o_ref[...] = acc.astype(o_ref.dtype)


def _sb_prompt(q, k, v, bias, *, batch, seq, tq=512, tk=256):
    h = SB_HEADS
    dh = q.shape[1] // h
    nq = seq // tq
    return pl.pallas_call(
        functools.partial(_sb_prompt_kernel, tq=tq, tk=tk, scale=dh ** -0.5),
        grid=(batch, h, nq),
        in_specs=[pl.BlockSpec((tq, dh), lambda b, hh, i: (b * nq + i, hh)),
                  pl.BlockSpec((seq, dh), lambda b, hh, i: (b, hh)),
                  pl.BlockSpec((seq, dh), lambda b, hh, i: (b, hh)),
                  pl.BlockSpec((1, 1, 1), lambda b, hh, i: (hh, 0, 0))],
        out_specs=pl.BlockSpec((tq, dh), lambda b, hh, i: (b * nq + i, hh)),
        out_shape=jax.ShapeDtypeStruct(q.shape, BF16),
        compiler_params=_params(("parallel", "parallel", "parallel")),
        name="sb_prompt",
    )(q, k, v, bias.reshape(h, 1, 1))


def _sb_decode_kernel(pt_ref, q_ref, bias_ref, kn_ref, vn_ref, *refs, pages_per_step, n_steps, page, heads,
                      scale, t_new):
    del pt_ref
    k_refs = refs[:pages_per_step]
    v_refs = refs[pages_per_step:2 * pages_per_step]
    o_ref, acc_ref, carry_ref = refs[2 * pages_per_step:]
    p = pl.program_id(1)
    hh = SUBLANES_F32
    n_groups = heads // hh
    rows = hh * t_new
    lanes = page * hh
    n_tiles = lanes // LANES
    dh = q_ref.shape[-1]
    scale2 = scale * LOG2_E

    rr = lax.broadcasted_iota(jnp.int32, (LANES, 2 * LANES), 0)
    cc = lax.broadcasted_iota(jnp.int32, (LANES, 2 * LANES), 1)
    suffix = jnp.where((cc >= LANES) | (rr // hh > cc // hh), 1.0, 0.0).astype(BF16)
    lane = lax.broadcasted_iota(jnp.int32, (rows, lanes), 1)
    row = lax.broadcasted_iota(jnp.int32, (rows, lanes), 0)
    own = lane % hh == row // t_new

    def rows_of(ref, grp):
        return ref[0, :, pl.ds(grp * hh, hh), :].reshape(lanes, dh).astype(BF16)

    def update(page_refs, masked):
        keep = own
        if masked:
            keep = own & (lane // hh < row % t_new)
        work = [(k_ref, v_ref, grp) for grp in range(n_groups) for k_ref, v_ref in page_refs]
        z2s = [lax.dot_general(q_ref[0, grp], rows_of(k_ref, grp), _NT, preferred_element_type=F32) * scale2
               + bias_ref[grp] * LOG2_E for k_ref, _, grp in work]
        sp2s = [jnp.where(keep, _softplus2(z2), 0.0) for z2 in z2s]
        sums = []
        for sp2 in sp2s:
            stacked = jnp.concatenate([sp2[:, j * LANES:(j + 1) * LANES] for j in range(n_tiles)], axis=0)
            sums.append(jnp.dot(stacked.astype(BF16), suffix, preferred_element_type=F32))
        ws = []
        run = None
        for i, (_, _, grp) in enumerate(work):
            if i % len(page_refs) == 0:
                run = carry_ref[grp]
            later = [None] * n_tiles
            for j in reversed(range(n_tiles)):
                tile_sums = sums[i][j * rows:(j + 1) * rows]
                later[j] = tile_sums[:, :LANES] + run
                run = run + tile_sums[:, LANES:]
            if (i + 1) % len(page_refs) == 0:
                carry_ref[grp] = run
            w = jnp.exp2(z2s[i] - sp2s[i] - jnp.concatenate(later, axis=-1))
            ws.append(jnp.where(keep, w, 0.0).astype(BF16))
        for grp in range(n_groups):
            acc = acc_ref[grp]
            for i, (_, v_ref, g2) in enumerate(work):
                if g2 == grp:
                    acc = acc + jnp.dot(ws[i], rows_of(v_ref, grp), preferred_element_type=F32)
            acc_ref[grp] = acc

    @pl.when(p == 0)
    def _():
        acc_ref[...] = jnp.zeros_like(acc_ref)
        carry_ref[...] = jnp.zeros_like(carry_ref)
        update([(kn_ref, vn_ref)], True)

    update(list(zip(k_refs, v_refs)), False)

    @pl.when(p == n_steps - 1)
    def _():
        o_ref[0] = acc_ref[...]


def _sb_decode(q, k_new, v_new, cache_k, cache_v, page_table, bias, *, batch, t_new, pages_per_step=4):
    h = SB_HEADS
    hh = SUBLANES_F32
    n_pool, page, _, dh = cache_k.shape
    assert h % hh == 0 and (page * hh) % LANES == 0 and LANES % hh == 0
    n_groups = h // hh
    rows = hh * t_new
    n_pages = page_table.shape[1]
    n_steps = n_pages // pages_per_step
    q_rows = jnp.transpose(q.reshape(batch, t_new, n_groups, hh, dh), (0, 2, 3, 1, 4))
    q_rows = q_rows.reshape(batch, n_groups, rows, dh).astype(BF16)
    bias_rows = jnp.repeat(bias.astype(F32).reshape(n_groups, hh), t_new, axis=1)[..., None]
    pad = lambda a: jnp.pad(a.reshape(batch, t_new, h, dh), ((0, 0), (0, page - t_new), (0, 0), (0, 0)))
    kn, vn = pad(k_new), pad(v_new)

    def page_map(g):
        return lambda b, p, pt: (pt[b, n_pages - 1 - (p * pages_per_step + g)], 0, 0, 0)

    page_specs = [pl.BlockSpec((1, page, h, dh), page_map(g)) for g in range(pages_per_step)]
    per_batch = lambda b, p, pt: (b, 0, 0, 0)
    acc = pl.pallas_call(
        functools.partial(_sb_decode_kernel, pages_per_step=pages_per_step, n_steps=n_steps, page=page, heads=h,
                          scale=dh ** -0.5, t_new=t_new),
        grid_spec=pltpu.PrefetchScalarGridSpec(
            num_scalar_prefetch=1,
            grid=(batch, n_steps),
            in_specs=[pl.BlockSpec((1, n_groups, rows, dh), per_batch),
                      pl.BlockSpec((n_groups, rows, 1), lambda b, p, pt: (0, 0, 0)),
                      pl.BlockSpec((1, page, h, dh), per_batch),
                      pl.BlockSpec((1, page, h, dh), per_batch)] + page_specs + page_specs,
            out_specs=pl.BlockSpec((1, n_groups, rows, dh), per_batch),
            scratch_shapes=[pltpu.VMEM((n_groups, rows, dh), F32), pltpu.VMEM((n_groups, rows, LANES), F32)],
        ),
        out_shape=jax.ShapeDtypeStruct((batch, n_groups, rows, dh), F32),
        compiler_params=_params(("parallel", "arbitrary")),
        name="sb_decode",
    )(page_table, q_rows, bias_rows, kn, vn, *([cache_k] * pages_per_step), *([cache_v] * pages_per_step))
    o = jnp.transpose(acc.reshape(batch, n_groups, hh, t_new, dh), (0, 3, 1, 2, 4))
    return o.reshape(batch * t_new, h * dh).astype(BF16)


def _trunk(x, weights, *, batch, seq, pos0, state0, decode):
    (w_ret_in, w_ret_out, w_kv, w_sb_q, w_sb_o, sb_bias, w_ff1, w_ff2, ln_g, ln_b) = weights
    m, d = x.shape
    g = lambda layer, i: ln_g[layer, i][None, :]
    b = lambda layer, i: ln_b[layer, i][None, :]

    proj = _proj(x, w_ret_in[0], [F32])[0]
    if seq % RET_CHUNK == 0:
        o, s_new = _retention(proj, state0, batch=batch, seq=seq, chunk=RET_CHUNK, n_valid=RET_CHUNK, pos0=pos0)
    else:
        chunk = SUBLANES_F32
        padded = jnp.pad(proj.reshape(batch, seq, -1), ((0, 0), (0, chunk - seq), (0, 0)))
        o, s_new = _retention(padded.reshape(batch * chunk, -1), state0, batch=batch, seq=chunk, chunk=chunk,
                              n_valid=seq, pos0=pos0)
        o = o.reshape(batch, chunk, -1)[:, :seq].reshape(m, -1)
    x, xb = _proj_norm(o, w_ret_out[0], x, g(0, 0), b(0, 0))
    x, xb = _mlp(x, xb, w_ff1[0], w_ff2[0], g(0, 1), b(0, 1))

    k_new, kb = _proj(xb, w_kv[0], [F32, BF16])
    v_new, vb = _proj(xb, w_kv[1], [F32, BF16])
    if decode is None:
        q = _proj(xb, w_sb_q[0], [BF16])[0]
        o = _sb_prompt(q, kb, vb, sb_bias[0], batch=batch, seq=seq)
    else:
        cache_k, cache_v, page_table = decode
        q = _proj(xb, w_sb_q[0], [F32])[0]
        o = _sb_decode(q, k_new, v_new, cache_k, cache_v, page_table, sb_bias[0], batch=batch, t_new=seq)
    x, xb = _proj_norm(o, w_sb_o[0], x, g(1, 0), b(1, 0))
    y, _ = _mlp(x, xb, w_ff1[1], w_ff2[1], g(1, 1), b(1, 1))
    return y, s_new, k_new, v_new


def kernel(x_prompt, x_sample, state_ret, cache_k, cache_v, page_table, w_ret_in, w_ret_out, w_kv, w_sb_q,
           w_sb_o, sb_bias, w_ff1, w_ff2, ln_g, ln_b):
    bp, tp, d = x_prompt.shape
    bs, ts, _ = x_sample.shape
    dh = d // SB_HEADS
    n_pages = page_table.shape[1]
    page = cache_k.shape[1]
    cast = lambda w: w.astype(BF16)
    weights = (cast(w_ret_in), cast(w_ret_out), (cast(w_kv[:, :d]), cast(w_kv[:, d:])), cast(w_sb_q),
               cast(w_sb_o), sb_bias,
               cast(w_ff1), cast(w_ff2), ln_g.astype(F32), ln_b.astype(F32))

    y_p, s_p, k_p, v_p = _trunk(x_prompt.reshape(bp * tp, d), weights, batch=bp, seq=tp, pos0=0,
                                state0=None, decode=None)
    y_s, s_s, k_s, v_s = _trunk(x_sample.reshape(bs * ts, d), weights, batch=bs, seq=ts, pos0=n_pages * page,
                                state0=state_ret[0].astype(F32), decode=(cache_k, cache_v, page_table))
    heads = lambda a, bb, tt: a.reshape(bb, tt, SB_HEADS, dh)
    return (y_p.reshape(bp, tp, d), y_s.reshape(bs, ts, d),
            s_p[None].astype(state_ret.dtype), heads(k_p, bp, tp), heads(v_p, bp, tp),
            s_s[None].astype(state_ret.dtype), heads(k_s, bs, ts), heads(v_s, bs, ts))
```

```python
import functools

import jax
import jax.numpy as jnp
from jax import lax
from jax.experimental import pallas as pl
from jax.experimental.pallas import tpu as pltpu

F32 = jnp.float32
BF16 = jnp.bfloat16

RET_HEADS = 8
RET_CHUNK = 128
ROPE_BASE = 10000.0
SB_HEADS = 16
LN_EPS = 1e-5
GN_EPS = 1e-6
DEPTH = 2
ALPHA = (2.0 * DEPTH) ** 0.25

V7X_VMEM_BYTES = 64 * 1024 * 1024
VMEM_LIMIT_BYTES = V7X_VMEM_BYTES - 12 * 1024 * 1024
LANES = 128
SUBLANES_F32 = 8
LOG2_E = 1.4426950408889634


def _params(semantics):
    return pltpu.CompilerParams(dimension_semantics=semantics, vmem_limit_bytes=VMEM_LIMIT_BYTES)


def _row_tile(m, want):
    return want if m % want == 0 else m


class _Weight:
    def __init__(self, array, layer=0, col0=0, ncols=None):
        self.array, self.layer, self.col0 = array, layer, col0
        self.rows = array.shape[-2]
        self.ncols = array.shape[-1] - col0 if ncols is None else ncols
        self.is_bf16 = array.dtype == BF16

    def spec(self, block, index_map):
        off = self.col0 // block[1]
        assert self.col0 % block[1] == 0
        if self.array.ndim == 2:
            return pl.BlockSpec(block, lambda *ids: (index_map(*ids)[0], index_map(*ids)[1] + off))
        return pl.BlockSpec((pl.Squeezed(),) + block,
                            lambda *ids: (self.layer, index_map(*ids)[0], index_map(*ids)[1] + off))


def _bf16(x):
    return x if x.dtype == BF16 else x.astype(BF16)


def _proj_kernel(x_ref, w_ref, *refs, n_out, cast_x, emit, scale):
    outs = refs[:n_out]
    rest = refs[n_out:]
    if cast_x:
        xb_ref = rest[-1]

        @pl.when(pl.program_id(1) == 0)
        def _():
            xb_ref[...] = x_ref[...].astype(BF16)

        xb = xb_ref[...]
    else:
        xb = x_ref[...]
    w = _bf16(w_ref[...])
    if emit:
        rest[0][...] = w
    acc = jnp.dot(xb, w, preferred_element_type=F32)
    if scale is not None:
        acc = acc * scale
    for o in outs:
        o[...] = acc.astype(o.dtype)


def _proj(x, w, out_dtypes, *, emit=False, scale=None, tm=1024, tn=512):
    m, k = x.shape
    n = w.ncols
    tm = _row_tile(m, tm)
    tn = _row_tile(n, tn)
    assert not emit or m == tm
    cast_x = x.dtype != BF16
    scratch = [pltpu.VMEM((tm, k), BF16)] if cast_x else []
    emit_specs = [pl.BlockSpec((k, tn), lambda i, j: (0, j))] if emit else []
    emit_shapes = [jax.ShapeDtypeStruct((k, n), BF16)] if emit else []
    return pl.pallas_call(
        functools.partial(_proj_kernel, n_out=len(out_dtypes), cast_x=cast_x, emit=emit, scale=scale),
        grid=(m // tm, n // tn),
        in_specs=[pl.BlockSpec((tm, k), lambda i, j: (i, 0)),
                  w.spec((k, tn), lambda i, j: (0, j))],
        out_specs=[pl.BlockSpec((tm, tn), lambda i, j: (i, j)) for _ in out_dtypes] + emit_specs,
        out_shape=[jax.ShapeDtypeStruct((m, n), dt) for dt in out_dtypes] + emit_shapes,
        scratch_shapes=scratch,
        compiler_params=_params(("parallel", "arbitrary")),
        name="proj",
    )(x, w.array)


def _post_norm_store(res, sub, g_ref, b_ref, of_ref, ob_ref):
    y = ALPHA * res + sub
    mu = jnp.mean(y, axis=-1, keepdims=True)
    yc = y - mu
    var = jnp.mean(yc * yc, axis=-1, keepdims=True)
    out = yc * lax.rsqrt(var + LN_EPS) * g_ref[...] + b_ref[...]
    of_ref[...] = out
    ob_ref[...] = out.astype(BF16)


ACC_COLS = 512


def _accumulate_dot(acc_ref, lhs, w_ref, step, wq_ref=None):
    @pl.when(step == 0)
    def _():
        acc_ref[...] = jnp.zeros_like(acc_ref)

    n = acc_ref.shape[1]
    tn = ACC_COLS if n % ACC_COLS == 0 else n
    for j in range(n // tn):
        cols = slice(j * tn, (j + 1) * tn)
        w = _bf16(w_ref[:, cols])
        if wq_ref is not None:
            wq_ref[:, cols] = w
        acc_ref[:, cols] += jnp.dot(lhs, w, preferred_element_type=F32)


def _proj_norm_kernel(x_ref, w_ref, res_ref, g_ref, b_ref, of_ref, ob_ref, *rest, nk, emit):
    wq_ref = rest[0] if emit else None
    if nk == 1:
        w = _bf16(w_ref[...])
        if emit:
            wq_ref[...] = w
        sub = jnp.dot(x_ref[...], w, preferred_element_type=F32)
        _post_norm_store(res_ref[...], sub, g_ref, b_ref, of_ref, ob_ref)
        return
    acc_ref = rest[-1]
    k = pl.program_id(1)
    _accumulate_dot(acc_ref, x_ref[...], w_ref, k, wq_ref)

    @pl.when(k == nk - 1)
    def _():
        _post_norm_store(res_ref[...], acc_ref[...], g_ref, b_ref, of_ref, ob_ref)


def _proj_norm(x, w, res, g, b, *, emit=False, tm=512):
    m, kdim = x.shape
    n = w.ncols
    tm = _row_tile(m, tm)
    assert not emit or m == tm
    tk = min(kdim, 2048 if w.is_bf16 else 1024)
    nk = kdim // tk
    row = lambda i, k: (i, 0)
    emit_specs = [pl.BlockSpec((tk, n), lambda i, k: (k, 0))] if emit else []
    emit_shapes = [jax.ShapeDtypeStruct((kdim, n), BF16)] if emit else []
    return pl.pallas_call(
        functools.partial(_proj_norm_kernel, nk=nk, emit=emit),
        grid=(m // tm, nk),
        in_specs=[pl.BlockSpec((tm, tk), lambda i, k: (i, k)),
                  w.spec((tk, n), lambda i, k: (k, 0)),
                  pl.BlockSpec((tm, n), row),
                  pl.BlockSpec((1, n), lambda i, k: (0, 0)),
                  pl.BlockSpec((1, n), lambda i, k: (0, 0))],
        out_specs=[pl.BlockSpec((tm, n), row), pl.BlockSpec((tm, n), row)] + emit_specs,
        out_shape=[jax.ShapeDtypeStruct((m, n), F32), jax.ShapeDtypeStruct((m, n), BF16)] + emit_shapes,
        scratch_shapes=[pltpu.VMEM((tm, n), F32)] if nk > 1 else [],
        compiler_params=_params(("parallel", "arbitrary")),
        name="proj_norm",
    )(x, w.array, res, g, b)


def _mlp_kernel(x_ref, xb_ref, w1_ref, w2_ref, g_ref, b_ref, of_ref, ob_ref, *rest, nf, emit):
    acc_ref = rest[-1]
    f = pl.program_id(1)
    w1 = _bf16(w1_ref[...])
    if emit:
        rest[0][...] = w1
    h = jnp.dot(xb_ref[...], w1, preferred_element_type=F32)
    h = jnp.square(jnp.maximum(h, 0.0)).astype(BF16)
    _accumulate_dot(acc_ref, h, w2_ref, f, rest[1] if emit else None)

    @pl.when(f == nf - 1)
    def _():
        _post_norm_store(x_ref[...], acc_ref[...], g_ref, b_ref, of_ref, ob_ref)


def _mlp(x, xb, w1, w2, g, b, *, emit=False, tm=512):
    m, d = x.shape
    dff = w1.ncols
    tm = _row_tile(m, tm)
    assert not emit or m == tm
    tf = 1024 if w1.is_bf16 else 512
    nf = dff // tf
    row = lambda i, f: (i, 0)
    emit_specs = [pl.BlockSpec((d, tf), lambda i, f: (0, f)), pl.BlockSpec((tf, d), lambda i, f: (f, 0))] if emit else []
    emit_shapes = [jax.ShapeDtypeStruct((d, dff), BF16), jax.ShapeDtypeStruct((dff, d), BF16)] if emit else []
    return pl.pallas_call(
        functools.partial(_mlp_kernel, nf=nf, emit=emit),
        grid=(m // tm, nf),
        in_specs=[pl.BlockSpec((tm, d), row),
                  pl.BlockSpec((tm, d), row),
                  w1.spec((d, tf), lambda i, f: (0, f)),
                  w2.spec((tf, d), lambda i, f: (f, 0)),
                  pl.BlockSpec((1, d), lambda i, f: (0, 0)),
                  pl.BlockSpec((1, d), lambda i, f: (0, 0))],
        out_specs=[pl.BlockSpec((tm, d), row), pl.BlockSpec((tm, d), row)] + emit_specs,
        out_shape=[jax.ShapeDtypeStruct((m, d), F32), jax.ShapeDtypeStruct((m, d), BF16)] + emit_shapes,
        scratch_shapes=[pltpu.VMEM((tm, d), F32)],
        compiler_params=_params(("parallel", "arbitrary")),
        name="mlp",
    )(x, xb, w1.array, w2.array, g, b)


RET_HEADS_PER_STEP = 4


def _rotary(x, cos, sin):
    half = x.shape[-1] // 2
    x1, x2 = x[:, :half], x[:, half:]
    return jnp.concatenate([x1 * cos - x2 * sin, x1 * sin + x2 * cos], axis=-1)


def _retention_kernel(*refs, n_chunks, has_state0, q_scale, hb, dk_dim, dv_dim):
    if has_state0:
        s0_ref, refs = refs[0], refs[1:]
    (q_ref, k_ref, v_ref, g_ref, cos_ref, sin_ref, din_ref, dq_ref, dk_ref, dc_ref,
     o_ref, sout_ref, s_ref) = refs
    c = pl.program_id(2)

    @pl.when(c == 0)
    def _():
        if has_state0:
            s_ref[...] = s0_ref[0]
        else:
            s_ref[...] = jnp.zeros_like(s_ref)

    cos, sin = cos_ref[...], sin_ref[...]
    for i in range(hb):
        kcols = slice(i * dk_dim, (i + 1) * dk_dim)
        vcols = slice(i * dv_dim, (i + 1) * dv_dim)
        q = (_rotary(q_ref[:, kcols], cos, sin) * q_scale).astype(BF16)
        k = _rotary(k_ref[:, kcols], cos, sin)
        v = v_ref[:, vcols].astype(BF16)
        state = s_ref[i]

        scores = lax.dot_general(q, k.astype(BF16), (((1,), (1,)), ((), ())),
                                 preferred_element_type=F32) * din_ref[i]
        inner = jnp.dot(scores.astype(BF16), v, preferred_element_type=F32)
        cross = jnp.dot(q, state.astype(BF16), preferred_element_type=F32) * dq_ref[i]
        kd = (k * dk_ref[i]).astype(BF16)
        s_ref[i] = dc_ref[i] * state + lax.dot_general(kd, v, (((0,), (0,)), ((), ())),
                                                       preferred_element_type=F32)

        o = inner + cross
        mu = jnp.mean(o, axis=-1, keepdims=True)
        oc = o - mu
        var = jnp.mean(oc * oc, axis=-1, keepdims=True)
        gate = g_ref[:, vcols]
        gate = gate / (1.0 + jnp.exp(-gate))
        o_ref[:, vcols] = (gate * (oc * lax.rsqrt(var + GN_EPS))).astype(o_ref.dtype)

    @pl.when(c == n_chunks - 1)
    def _():
        sout_ref[0] = s_ref[...]


def _retention_tables(chunk, n_valid):
    h = RET_HEADS
    log_g = jnp.log1p(-jnp.power(2.0, -5.0 - jnp.arange(h, dtype=F32)))
    idx = jnp.arange(chunk, dtype=F32)
    diff = idx[:, None] - idx[None, :]
    din = jnp.where(diff[None] >= 0, jnp.exp(log_g[:, None, None] * jnp.maximum(diff, 0.0)[None]), 0.0)
    dq = jnp.exp(log_g[:, None] * (idx + 1.0)[None, :])[..., None]
    dk = jnp.exp(log_g[:, None] * (n_valid - 1.0 - idx)[None, :])[..., None]
    dc = jnp.exp(log_g * n_valid).reshape(h, 1, 1)
    return din, dq, dk, dc


def _rotary_tables(pos, half):
    inv = ROPE_BASE ** (-jnp.arange(half, dtype=F32) / half)
    ang = pos.astype(F32)[:, None] * inv[None, :]
    return jnp.cos(ang), jnp.sin(ang)


def _retention(proj, state0, *, batch, seq, chunk, n_valid, pos0):
    h = RET_HEADS
    hb = RET_HEADS_PER_STEP
    ng = h // hb
    width = proj.shape[1]
    dk_dim = width // (6 * h)
    dv_dim = 2 * dk_dim
    n_chunks = seq // chunk
    cos, sin = _rotary_tables(pos0 + jnp.arange(seq, dtype=jnp.int32), dk_dim // 2)
    din, dq, dk, dc = _retention_tables(chunk, n_valid)
    has_state0 = state0 is not None

    row = lambda b, hg, c: b * n_chunks + c
    per_group = lambda b, hg, c: (hg, 0, 0)
    in_specs = [
        pl.BlockSpec((chunk, hb * dk_dim), lambda b, hg, c: (row(b, hg, c), hg)),
        pl.BlockSpec((chunk, hb * dk_dim), lambda b, hg, c: (row(b, hg, c), ng + hg)),
        pl.BlockSpec((chunk, hb * dv_dim), lambda b, hg, c: (row(b, hg, c), ng + hg)),
        pl.BlockSpec((chunk, hb * dv_dim), lambda b, hg, c: (row(b, hg, c), 2 * ng + hg)),
        pl.BlockSpec((chunk, dk_dim // 2), lambda b, hg, c: (c, 0)),
        pl.BlockSpec((chunk, dk_dim // 2), lambda b, hg, c: (c, 0)),
        pl.BlockSpec((hb, chunk, chunk), per_group),
        pl.BlockSpec((hb, chunk, 1), per_group),
        pl.BlockSpec((hb, chunk, 1), per_group),
        pl.BlockSpec((hb, 1, 1), per_group),
    ]
    args = [proj, proj, proj, proj, cos, sin, din, dq, dk, dc]
    state_spec = pl.BlockSpec((1, hb, dk_dim, dv_dim), lambda b, hg, c: (b, hg, 0, 0))
    if has_state0:
        in_specs.insert(0, state_spec)
        args.insert(0, state0)
    o, s_out = pl.pallas_call(
        functools.partial(_retention_kernel, n_chunks=n_chunks, has_state0=has_state0,
                          q_scale=dk_dim ** -0.5, hb=hb, dk_dim=dk_dim, dv_dim=dv_dim),
        grid=(batch, ng, n_chunks),
        in_specs=in_specs,
        out_specs=[pl.BlockSpec((chunk, hb * dv_dim), lambda b, hg, c: (row(b, hg, c), hg)), state_spec],
        out_shape=[jax.ShapeDtypeStruct((batch * seq, h * dv_dim), BF16),
                   jax.ShapeDtypeStruct((batch, h, dk_dim, dv_dim), F32)],
        scratch_shapes=[pltpu.VMEM((hb, dk_dim, dv_dim), F32)],
        compiler_params=_params(("parallel", "parallel", "arbitrary")),
        name="retention",
    )(*args)
    return o, s_out


def _softplus2(z2):
    return jnp.maximum(z2, 0.0) + jnp.log2(1.0 + jnp.exp2(-jnp.abs(z2)))


_NT = (((1,), (1,)), ((), ()))


def _sb_prompt_kernel(q_ref, k_ref, v_ref, bias_ref, o_ref, *, tq, tk):
    qi = pl.program_id(2)
    q = q_ref[...]
    bias2 = bias_ref[0] * LOG2_E
    nsub = tq // tk
    r = lax.broadcasted_iota(jnp.int32, (tk, tk), 0)
    c = lax.broadcasted_iota(jnp.int32, (tk, tk), 1)
    suffix = jnp.where(r > c, 1.0, 0.0).astype(BF16)
    qrow = lax.broadcasted_iota(jnp.int32, (tq, tk), 0)
    kcol = lax.broadcasted_iota(jnp.int32, (tq, tk), 1)

    def process(chunks, diagonal, state):
        carry, acc = state
        starts = [pl.multiple_of(j * tq, tq) for j in chunks]
        raws = [lax.dot_general(q, k_ref[pl.ds(st, tq), :], _NT, preferred_element_type=F32) for st in starts]
        logits, sp2_bf, totals, visibles = [], [], [], []
        for raw in raws:
            for s in reversed(range(nsub)):
                z2 = raw[:, s * tk:(s + 1) * tk] + bias2
                sp2 = _softplus2(z2)
                if diagonal:
                    visibles.append(kcol + s * tk < qrow)
                    sp2 = jnp.where(visibles[-1], sp2, 0.0)
                logits.append(z2 - sp2)
                sp2_bf.append(sp2.astype(BF16))
                totals.append(jnp.sum(sp2, axis=-1, keepdims=True))
        sums = [jnp.dot(x, suffix, preferred_element_type=F32) for x in sp2_bf]
        ws = []
        for i, (logit, in_block, total) in enumerate(zip(logits, sums, totals)):
            w = jnp.exp2(logit - (in_block + carry))
            if diagonal:
                w = jnp.where(visibles[i], w, 0.0)
            ws.append(w.astype(BF16))
            carry = carry + total
        for n, st in enumerate(starts):
            w_chunk = jnp.concatenate(ws[n * nsub:(n + 1) * nsub][::-1], axis=-1)
            v_chunk = v_ref[pl.ds(st, tq), :]
            half = tq // 2
            acc = acc + jnp.concatenate(
                [jnp.dot(w_chunk[:half], v_chunk, preferred_element_type=F32),
                 jnp.dot(w_chunk[half:], v_chunk, preferred_element_type=F32)], axis=0)
        return carry, acc

    state = (jnp.zeros((tq, 1), F32), jnp.zeros((tq, q.shape[-1]), F32))
    state = process([qi], True, state)
    odd = qi % 2
    state = lax.fori_loop(0, odd, lambda it, st: process([qi - 1], False, st), state)
    first = qi - 1 - odd

    def pair(it, st):
        j = first - 2 * it
        return process([j, j - 1], False, st)

    _, acc = lax.fori_loop(0, qi // 2, pair, state)
    o_ref[...] = acc.astype(o_ref.dtype)


def _sb_prompt(q, k, v, bias, *, batch, seq, tq=512, tk=256):
    h = SB_HEADS
    dh = q.shape[1] // h
    nq = seq // tq
    return pl.pallas_call(
        functools.partial(_sb_prompt_kernel, tq=tq, tk=tk),
        grid=(batch, h, nq),
        in_specs=[pl.BlockSpec((tq, dh), lambda b, hh, i: (b * nq + i, hh)),
                  pl.BlockSpec((seq, dh), lambda b, hh, i: (b, hh)),
                  pl.BlockSpec((seq, dh), lambda b, hh, i: (b, hh)),
                  pl.BlockSpec((1, 1, 1), lambda b, hh, i: (hh, 0, 0))],
        out_specs=pl.BlockSpec((tq, dh), lambda b, hh, i: (b * nq + i, hh)),
        out_shape=jax.ShapeDtypeStruct(q.shape, BF16),
        compiler_params=_params(("parallel", "parallel", "parallel")),
        name="sb_prompt",
    )(q, k, v, bias.reshape(h, 1, 1))


def _sb_decode_kernel(pt_ref, q_ref, bias_ref, kn_ref, vn_ref, *refs, pages_per_step, n_steps, page, heads,
                      t_new):
    del pt_ref
    k_refs = refs[:pages_per_step]
    v_refs = refs[pages_per_step:2 * pages_per_step]
    o_ref, acc_ref, carry_ref = refs[2 * pages_per_step:]
    p = pl.program_id(1)
    hh = SUBLANES_F32
    n_groups = heads // hh
    rows = hh * t_new
    lanes = page * hh
    n_tiles = lanes // LANES
    dh = q_ref.shape[-1]

    rr = lax.broadcasted_iota(jnp.int32, (LANES, 2 * LANES), 0)
    cc = lax.broadcasted_iota(jnp.int32, (LANES, 2 * LANES), 1)
    suffix = jnp.where((cc >= LANES) | (rr // hh > cc // hh), 1.0, 0.0).astype(BF16)
    lane = lax.broadcasted_iota(jnp.int32, (rows, lanes), 1)
    row = lax.broadcasted_iota(jnp.int32, (rows, lanes), 0)
    own = lane % hh == row // t_new

    def rows_of(ref, grp):
        return ref[0, :, pl.ds(grp * hh, hh), :].reshape(lanes, dh).astype(BF16)

    def update(page_refs, masked):
        keep = own
        if masked:
            keep = own & (lane // hh < row % t_new)
        work = [(k_ref, v_ref, grp) for grp in range(n_groups) for k_ref, v_ref in page_refs]
        z2s = [lax.dot_general(q_ref[0, grp], rows_of(k_ref, grp), _NT, preferred_element_type=F32)
               + bias_ref[grp] * LOG2_E for k_ref, _, grp in work]
        sp2s = [jnp.where(keep, _softplus2(z2), 0.0) for z2 in z2s]
        sums = []
        for sp2 in sp2s:
            stacked = jnp.concatenate([sp2[:, j * LANES:(j + 1) * LANES] for j in range(n_tiles)], axis=0)
            sums.append(jnp.dot(stacked.astype(BF16), suffix, preferred_element_type=F32))
        ws = []
        run = None
        for i, (_, _, grp) in enumerate(work):
            if i % len(page_refs) == 0:
                run = carry_ref[grp]
            later = [None] * n_tiles
            for j in reversed(range(n_tiles)):
                tile_sums = sums[i][j * rows:(j + 1) * rows]
                later[j] = tile_sums[:, :LANES] + run
                run = run + tile_sums[:, LANES:]
            if (i + 1) % len(page_refs) == 0:
                carry_ref[grp] = run
            w = jnp.exp2(z2s[i] - sp2s[i] - jnp.concatenate(later, axis=-1))
            ws.append(jnp.where(keep, w, 0.0).astype(BF16))
        for grp in range(n_groups):
            acc = acc_ref[grp]
            for i, (_, v_ref, g2) in enumerate(work):
                if g2 == grp:
                    acc = acc + jnp.dot(ws[i], rows_of(v_ref, grp), preferred_element_type=F32)
            acc_ref[grp] = acc

    @pl.when(p == 0)
    def _():
        acc_ref[...] = jnp.zeros_like(acc_ref)
        carry_ref[...] = jnp.zeros_like(carry_ref)
        update([(kn_ref, vn_ref)], True)

    update(list(zip(k_refs, v_refs)), False)

    @pl.when(p == n_steps - 1)
    def _():
        o_ref[0] = acc_ref[...]


def _sb_decode(q, k_new, v_new, cache_k, cache_v, page_table, bias, *, batch, t_new, pages_per_step=8):
    h = SB_HEADS
    hh = SUBLANES_F32
    n_pool, page, _, dh = cache_k.shape
    assert h % hh == 0 and (page * hh) % LANES == 0 and LANES % hh == 0
    n_groups = h // hh
    rows = hh * t_new
    n_pages = page_table.shape[1]
    n_steps = n_pages // pages_per_step
    q_rows = jnp.transpose(q.reshape(batch, t_new, n_groups, hh, dh), (0, 2, 3, 1, 4))
    q_rows = q_rows.reshape(batch, n_groups, rows, dh).astype(BF16)
    bias_rows = jnp.repeat(bias.astype(F32).reshape(n_groups, hh), t_new, axis=1)[..., None]
    pad = lambda a: jnp.pad(a.reshape(batch, t_new, h, dh), ((0, 0), (0, page - t_new), (0, 0), (0, 0)))
    kn, vn = pad(k_new), pad(v_new)

    def page_map(g):
        return lambda b, p, pt: (pt[b, n_pages - 1 - (p * pages_per_step + g)], 0, 0, 0)

    page_specs = [pl.BlockSpec((1, page, h, dh), page_map(g)) for g in range(pages_per_step)]
    per_batch = lambda b, p, pt: (b, 0, 0, 0)
    acc = pl.pallas_call(
        functools.partial(_sb_decode_kernel, pages_per_step=pages_per_step, n_steps=n_steps, page=page, heads=h,
                          t_new=t_new),
        grid_spec=pltpu.PrefetchScalarGridSpec(
            num_scalar_prefetch=1,
            grid=(batch, n_steps),
            in_specs=[pl.BlockSpec((1, n_groups, rows, dh), per_batch),
                      pl.BlockSpec((n_groups, rows, 1), lambda b, p, pt: (0, 0, 0)),
                      pl.BlockSpec((1, page, h, dh), per_batch),
                      pl.BlockSpec((1, page, h, dh), per_batch)] + page_specs + page_specs,
            out_specs=pl.BlockSpec((1, n_groups, rows, dh), per_batch),
            scratch_shapes=[pltpu.VMEM((n_groups, rows, dh), F32), pltpu.VMEM((n_groups, rows, LANES), F32)],
        ),
        out_shape=jax.ShapeDtypeStruct((batch, n_groups, rows, dh), F32),
        compiler_params=_params(("parallel", "arbitrary")),
        name="sb_decode",
    )(page_table, q_rows, bias_rows, kn, vn, *([cache_k] * pages_per_step), *([cache_v] * pages_per_step))
    o = jnp.transpose(acc.reshape(batch, n_groups, hh, t_new, dh), (0, 3, 1, 2, 4))
    return o.reshape(batch * t_new, h * dh).astype(BF16)


def _trunk(x, w, norms, sb_bias, *, batch, seq, pos0, state0, decode, emit):
    ln_g, ln_b = norms
    m, d = x.shape
    g = lambda layer, i: ln_g[layer, i][None, :]
    b = lambda layer, i: ln_b[layer, i][None, :]
    copies = {}

    def keep(name, outs):
        if emit:
            copies[name] = _Weight(outs[-1])
            return outs[:-1]
        return outs

    proj, = keep("ret_in", _proj(x, w["ret_in"], [F32], emit=emit))
    if seq % RET_CHUNK == 0:
        o, s_new = _retention(proj, state0, batch=batch, seq=seq, chunk=RET_CHUNK, n_valid=RET_CHUNK, pos0=pos0)
    else:
        chunk = SUBLANES_F32
        padded = jnp.pad(proj.reshape(batch, seq, -1), ((0, 0), (0, chunk - seq), (0, 0)))
        o, s_new = _retention(padded.reshape(batch * chunk, -1), state0, batch=batch, seq=chunk, chunk=chunk,
                              n_valid=seq, pos0=pos0)
        o = o.reshape(batch, chunk, -1)[:, :seq].reshape(m, -1)
    x, xb = keep("ret_out", _proj_norm(o, w["ret_out"], x, g(0, 0), b(0, 0), emit=emit))
    outs = _mlp(x, xb, w["ff1_0"], w["ff2_0"], g(0, 1), b(0, 1), emit=emit)
    if emit:
        copies["ff1_0"], copies["ff2_0"] = _Weight(outs[2]), _Weight(outs[3])
    x, xb = outs[:2]

    k_new, kb = keep("k", _proj(xb, w["k"], [F32, BF16], emit=emit))
    v_new, vb = keep("v", _proj(xb, w["v"], [F32, BF16], emit=emit))
    q_scale = LOG2_E * (d // SB_HEADS) ** -0.5
    if decode is None:
        q, = keep("sb_q", _proj(xb, w["sb_q"], [BF16], emit=emit, scale=q_scale))
        o = _sb_prompt(q, kb, vb, sb_bias, batch=batch, seq=seq)
    else:
        cache_k, cache_v, page_table = decode
        q, = keep("sb_q", _proj(xb, w["sb_q"], [F32], emit=emit, scale=q_scale))
        o = _sb_decode(q, k_new, v_new, cache_k, cache_v, page_table, sb_bias, batch=batch, t_new=seq)
    x, xb = keep("sb_o", _proj_norm(o, w["sb_o"], x, g(1, 0), b(1, 0), emit=emit))
    outs = _mlp(x, xb, w["ff1_1"], w["ff2_1"], g(1, 1), b(1, 1), emit=emit)
    if emit:
        copies["ff1_1"], copies["ff2_1"] = _Weight(outs[2]), _Weight(outs[3])
    return outs[0], s_new, k_new, v_new, copies


def kernel(x_prompt, x_sample, state_ret, cache_k, cache_v, page_table, w_ret_in, w_ret_out, w_kv, w_sb_q,
           w_sb_o, sb_bias, w_ff1, w_ff2, ln_g, ln_b):
    bp, tp, d = x_prompt.shape
    bs, ts, _ = x_sample.shape
    dh = d // SB_HEADS
    n_pages = page_table.shape[1]
    page = cache_k.shape[1]
    weights = {"ret_in": _Weight(w_ret_in, 0), "ret_out": _Weight(w_ret_out, 0),
               "k": _Weight(w_kv, col0=0, ncols=d), "v": _Weight(w_kv, col0=d, ncols=d),
               "sb_q": _Weight(w_sb_q, 0), "sb_o": _Weight(w_sb_o, 0),
               "ff1_0": _Weight(w_ff1, 0), "ff2_0": _Weight(w_ff2, 0),
               "ff1_1": _Weight(w_ff1, 1), "ff2_1": _Weight(w_ff2, 1)}
    norms = (ln_g.astype(F32), ln_b.astype(F32))

    y_s, s_s, k_s, v_s, weights_bf16 = _trunk(
        x_sample.reshape(bs * ts, d), weights, norms, sb_bias[0], batch=bs, seq=ts, pos0=n_pages * page,
        state0=state_ret[0].astype(F32), decode=(cache_k, cache_v, page_table), emit=True)
    y_p, s_p, k_p, v_p, _ = _trunk(
        x_prompt.reshape(bp * tp, d), weights_bf16, norms, sb_bias[0], batch=bp, seq=tp, pos0=0,
        state0=None, decode=None, emit=False)
    heads = lambda a, bb, tt: a.reshape(bb, tt, SB_HEADS, dh)
    return (y_p.reshape(bp, tp, d), y_s.reshape(bs, ts, d),
            s_p[None].astype(state_ret.dtype), heads(k_p, bp, tp), heads(v_p, bp, tp),
            s_s[None].astype(state_ret.dtype), heads(k_s, bs, ts), heads(v_s, bs, ts))
```

```python
import functools

import jax
import jax.numpy as jnp
from jax import lax
from jax.experimental import pallas as pl
from jax.experimental.pallas import tpu as pltpu

F32 = jnp.float32
BF16 = jnp.bfloat16

RET_HEADS = 8
RET_CHUNK = 128
ROPE_BASE = 10000.0
SB_HEADS = 16
LN_EPS = 1e-5
GN_EPS = 1e-6
DEPTH = 2
ALPHA = (2.0 * DEPTH) ** 0.25

V7X_VMEM_BYTES = 64 * 1024 * 1024
VMEM_LIMIT_BYTES = V7X_VMEM_BYTES - 12 * 1024 * 1024
LANES = 128
SUBLANES_F32 = 8
LOG2_E = 1.4426950408889634


def _params(semantics):
    return pltpu.CompilerParams(dimension_semantics=semantics, vmem_limit_bytes=VMEM_LIMIT_BYTES)


def _row_tile(m, want):
    return want if m % want == 0 else m


class _Weight:
    def __init__(self, array, layer=0, col0=0, ncols=None):
        self.array, self.layer, self.col0 = array, layer, col0
        self.rows = array.shape[-2]
        self.ncols = array.shape[-1] - col0 if ncols is None else ncols
        self.is_bf16 = array.dtype == BF16

    def spec(self, block, index_map):
        off = self.col0 // block[1]
        assert self.col0 % block[1] == 0
        if self.array.ndim == 2:
            return pl.BlockSpec(block, lambda *ids: (index_map(*ids)[0], index_map(*ids)[1] + off))
        return pl.BlockSpec((pl.Squeezed(),) + block,
                            lambda *ids: (self.layer, index_map(*ids)[0], index_map(*ids)[1] + off))


def _bf16(x):
    return x if x.dtype == BF16 else x.astype(BF16)


def _proj_kernel(x_ref, w_ref, *refs, n_out, cast_x, emit, scale):
    outs = refs[:n_out]
    rest = refs[n_out:]
    if cast_x:
        xb_ref = rest[-1]

        @pl.when(pl.program_id(1) == 0)
        def _():
            xb_ref[...] = x_ref[...].astype(BF16)

        xb = xb_ref[...]
    else:
        xb = x_ref[...]
    w = _bf16(w_ref[...])
    if emit:
        rest[0][...] = w
    acc = jnp.dot(xb, w, preferred_element_type=F32)
    if scale is not None:
        acc = acc * scale
    for o in outs:
        o[...] = acc.astype(o.dtype)


def _proj(x, w, out_dtypes, *, emit=False, scale=None, tm=1024, tn=1024):
    m, k = x.shape
    n = w.ncols
    tm = _row_tile(m, tm)
    tn = _row_tile(n, tn)
    assert not emit or m == tm
    cast_x = x.dtype != BF16
    scratch = [pltpu.VMEM((tm, k), BF16)] if cast_x else []
    emit_specs = [pl.BlockSpec((k, tn), lambda i, j: (0, j))] if emit else []
    emit_shapes = [jax.ShapeDtypeStruct((k, n), BF16)] if emit else []
    return pl.pallas_call(
        functools.partial(_proj_kernel, n_out=len(out_dtypes), cast_x=cast_x, emit=emit, scale=scale),
        grid=(m // tm, n // tn),
        in_specs=[pl.BlockSpec((tm, k), lambda i, j: (i, 0)),
                  w.spec((k, tn), lambda i, j: (0, j))],
        out_specs=[pl.BlockSpec((tm, tn), lambda i, j: (i, j)) for _ in out_dtypes] + emit_specs,
        out_shape=[jax.ShapeDtypeStruct((m, n), dt) for dt in out_dtypes] + emit_shapes,
        scratch_shapes=scratch,
        compiler_params=_params(("parallel", "arbitrary")),
        name="proj",
    )(x, w.array)


def _proj_kvq_kernel(x_ref, wk_ref, wv_ref, wq_ref, kf_ref, kb_ref, vf_ref, vb_ref, q_ref, *, tiles, q_scale):
    j = pl.program_id(1)
    x = x_ref[...]

    @pl.when(j < tiles)
    def _():
        acc = jnp.dot(x, wk_ref[...], preferred_element_type=F32)
        kf_ref[...] = acc
        kb_ref[...] = acc.astype(BF16)

    @pl.when((j >= tiles) & (j < 2 * tiles))
    def _():
        acc = jnp.dot(x, wv_ref[...], preferred_element_type=F32)
        vf_ref[...] = acc
        vb_ref[...] = acc.astype(BF16)

    @pl.when(j >= 2 * tiles)
    def _():
        q_ref[...] = (jnp.dot(x, wq_ref[...], preferred_element_type=F32) * q_scale).astype(BF16)


def _proj_kvq(xb, wk, wv, wq, q_scale, *, tm=1024, tn=512):
    m, k = xb.shape
    n = wk.ncols
    assert wk.is_bf16 and wv.is_bf16 and wq.is_bf16 and wv.ncols == n and wq.ncols == n
    tiles = n // tn

    def tile_of(which):
        return lambda i, j: (i, jnp.clip(j - which * tiles, 0, tiles - 1))

    w_specs = [w.spec((k, tn), lambda i, j, t=tile_of(which): (0, t(i, j)[1]))
               for which, w in enumerate((wk, wv, wq))]
    out_spec = lambda which: pl.BlockSpec((tm, tn), tile_of(which))
    shape = lambda dt: jax.ShapeDtypeStruct((m, n), dt)
    return pl.pallas_call(
        functools.partial(_proj_kvq_kernel, tiles=tiles, q_scale=q_scale),
        grid=(m // tm, 3 * tiles),
        in_specs=[pl.BlockSpec((tm, k), lambda i, j: (i, 0))] + w_specs,
        out_specs=[out_spec(0), out_spec(0), out_spec(1), out_spec(1), out_spec(2)],
        out_shape=[shape(F32), shape(BF16), shape(F32), shape(BF16), shape(BF16)],
        compiler_params=_params(("parallel", "arbitrary")),
        name="proj_kvq",
    )(xb, wk.array, wv.array, wq.array)


def _post_norm_store(res, sub, g_ref, b_ref, of_ref, ob_ref):
    y = ALPHA * res + sub
    mu = jnp.mean(y, axis=-1, keepdims=True)
    yc = y - mu
    var = jnp.mean(yc * yc, axis=-1, keepdims=True)
    out = yc * lax.rsqrt(var + LN_EPS) * g_ref[...] + b_ref[...]
    of_ref[...] = out
    ob_ref[...] = out.astype(BF16)


ACC_COLS = 512


def _zero_on_first(acc_ref, step):
    @pl.when(step == 0)
    def _():
        acc_ref[...] = jnp.zeros_like(acc_ref)


def _accumulate_dot(acc_ref, lhs, w_ref, wq_ref=None):
    n = acc_ref.shape[1]
    tn = ACC_COLS if n % ACC_COLS == 0 else n
    for j in range(n // tn):
        cols = slice(j * tn, (j + 1) * tn)
        w = _bf16(w_ref[:, cols])
        if wq_ref is not None:
            wq_ref[:, cols] = w
        acc_ref[:, cols] += jnp.dot(lhs, w, preferred_element_type=F32)


def _proj_norm_kernel(x_ref, w_ref, res_ref, g_ref, b_ref, of_ref, ob_ref, *rest, nk, emit):
    wq_ref = rest[0] if emit else None
    if nk == 1:
        w = _bf16(w_ref[...])
        if emit:
            wq_ref[...] = w
        sub = jnp.dot(x_ref[...], w, preferred_element_type=F32)
        _post_norm_store(res_ref[...], sub, g_ref, b_ref, of_ref, ob_ref)
        return
    acc_ref = rest[-1]
    k = pl.program_id(1)
    _zero_on_first(acc_ref, k)
    _accumulate_dot(acc_ref, x_ref[...], w_ref, wq_ref)

    @pl.when(k == nk - 1)
    def _():
        _post_norm_store(res_ref[...], acc_ref[...], g_ref, b_ref, of_ref, ob_ref)


def _proj_norm(x, w, res, g, b, *, emit=False, tm=512):
    m, kdim = x.shape
    n = w.ncols
    tm = _row_tile(m, tm)
    assert not emit or m == tm
    tk = kdim if w.is_bf16 else min(kdim, 1024)
    nk = kdim // tk
    row = lambda i, k: (i, 0)
    emit_specs = [pl.BlockSpec((tk, n), lambda i, k: (k, 0))] if emit else []
    emit_shapes = [jax.ShapeDtypeStruct((kdim, n), BF16)] if emit else []
    w_spec = w.spec((tk, n), lambda i, k: (k, 0))
    if nk == 1:
        w_spec = pl.BlockSpec(w_spec.block_shape, w_spec.index_map, pipeline_mode=pl.Buffered(1))
    return pl.pallas_call(
        functools.partial(_proj_norm_kernel, nk=nk, emit=emit),
        grid=(m // tm, nk),
        in_specs=[pl.BlockSpec((tm, tk), lambda i, k: (i, k)),
                  w_spec,
                  pl.BlockSpec((tm, n), row),
                  pl.BlockSpec((1, n), lambda i, k: (0, 0)),
                  pl.BlockSpec((1, n), lambda i, k: (0, 0))],
        out_specs=[pl.BlockSpec((tm, n), row), pl.BlockSpec((tm, n), row)] + emit_specs,
        out_shape=[jax.ShapeDtypeStruct((m, n), F32), jax.ShapeDtypeStruct((m, n), BF16)] + emit_shapes,
        scratch_shapes=[pltpu.VMEM((tm, n), F32)] if nk > 1 else [],
        compiler_params=_params(("parallel", "arbitrary")),
        name="proj_norm",
    )(x, w.array, res, g, b)


def _mlp_kernel(x_ref, xb_ref, w1_ref, w2_ref, g_ref, b_ref, of_ref, ob_ref, *rest, nf, emit):
    acc_ref = rest[-1]
    f = pl.program_id(1)
    _zero_on_first(acc_ref, f)
    w1 = _bf16(w1_ref[...])
    if emit:
        rest[0][...] = w1
    h = jnp.dot(xb_ref[...], w1, preferred_element_type=F32)
    h = jnp.square(jnp.maximum(h, 0.0)).astype(BF16)
    _accumulate_dot(acc_ref, h, w2_ref, rest[1] if emit else None)

    @pl.when(f == nf - 1)
    def _():
        _post_norm_store(x_ref[...], acc_ref[...], g_ref, b_ref, of_ref, ob_ref)


def _mlp(x, xb, w1, w2, g, b, *, emit=False, tm=512):
    m, d = x.shape
    dff = w1.ncols
    tm = _row_tile(m, tm)
    assert not emit or m == tm
    tf = 1024 if w1.is_bf16 else 512
    nf = dff // tf
    row = lambda i, f: (i, 0)
    emit_specs = [pl.BlockSpec((d, tf), lambda i, f: (0, f)), pl.BlockSpec((tf, d), lambda i, f: (f, 0))] if emit else []
    emit_shapes = [jax.ShapeDtypeStruct((d, dff), BF16), jax.ShapeDtypeStruct((dff, d), BF16)] if emit else []
    return pl.pallas_call(
        functools.partial(_mlp_kernel, nf=nf, emit=emit),
        grid=(m // tm, nf),
        in_specs=[pl.BlockSpec((tm, d), row),
                  pl.BlockSpec((tm, d), row),
                  w1.spec((d, tf), lambda i, f: (0, f)),
                  w2.spec((tf, d), lambda i, f: (f, 0)),
                  pl.BlockSpec((1, d), lambda i, f: (0, 0)),
                  pl.BlockSpec((1, d), lambda i, f: (0, 0))],
        out_specs=[pl.BlockSpec((tm, d), row), pl.BlockSpec((tm, d), row)] + emit_specs,
        out_shape=[jax.ShapeDtypeStruct((m, d), F32), jax.ShapeDtypeStruct((m, d), BF16)] + emit_shapes,
        scratch_shapes=[pltpu.VMEM((tm, d), F32)],
        compiler_params=_params(("parallel", "arbitrary")),
        name="mlp",
    )(x, xb, w1.array, w2.array, g, b)


RET_HEADS_PER_STEP = 8


def _rotary(x, cos, sin):
    half = x.shape[-1] // 2
    x1, x2 = x[:, :half], x[:, half:]
    return jnp.concatenate([x1 * cos - x2 * sin, x1 * sin + x2 * cos], axis=-1)


def _retention_kernel(*refs, n_chunks, has_state0, q_scale, hb, dk_dim, dv_dim):
    if has_state0:
        s0_ref, refs = refs[0], refs[1:]
    (q_ref, k_ref, v_ref, g_ref, cos_ref, sin_ref, din_ref, dq_ref, dk_ref, dc_ref,
     o_ref, sout_ref, s_ref) = refs
    c = pl.program_id(2)

    @pl.when(c == 0)
    def _():
        if has_state0:
            s_ref[...] = s0_ref[0]
        else:
            s_ref[...] = jnp.zeros_like(s_ref)

    cos, sin = cos_ref[...], sin_ref[...]
    for i in range(hb):
        kcols = slice(i * dk_dim, (i + 1) * dk_dim)
        vcols = slice(i * dv_dim, (i + 1) * dv_dim)
        q = (_rotary(q_ref[:, kcols], cos, sin) * q_scale).astype(BF16)
        k = _rotary(k_ref[:, kcols], cos, sin)
        v = v_ref[:, vcols].astype(BF16)
        state = s_ref[i]

        scores = lax.dot_general(q, k.astype(BF16), (((1,), (1,)), ((), ())),
                                 preferred_element_type=F32) * din_ref[i]
        inner = jnp.dot(scores.astype(BF16), v, preferred_element_type=F32)
        cross = jnp.dot(q, state.astype(BF16), preferred_element_type=F32) * dq_ref[i]
        kd = (k * dk_ref[i]).astype(BF16)
        s_ref[i] = dc_ref[i] * state + lax.dot_general(kd, v, (((0,), (0,)), ((), ())),
                                                       preferred_element_type=F32)

        o = inner + cross
        mu = jnp.mean(o, axis=-1, keepdims=True)
        oc = o - mu
        var = jnp.mean(oc * oc, axis=-1, keepdims=True)
        gate = g_ref[:, vcols]
        gate = gate / (1.0 + jnp.exp(-gate))
        o_ref[:, vcols] = (gate * (oc * lax.rsqrt(var + GN_EPS))).astype(o_ref.dtype)

    @pl.when(c == n_chunks - 1)
    def _():
        sout_ref[0] = s_ref[...]


def _retention_tables(chunk, n_valid):
    h = RET_HEADS
    log_g = jnp.log1p(-jnp.power(2.0, -5.0 - jnp.arange(h, dtype=F32)))
    idx = jnp.arange(chunk, dtype=F32)
    diff = idx[:, None] - idx[None, :]
    din = jnp.where(diff[None] >= 0, jnp.exp(log_g[:, None, None] * jnp.maximum(diff, 0.0)[None]), 0.0)
    dq = jnp.exp(log_g[:, None] * (idx + 1.0)[None, :])[..., None]
    dk = jnp.exp(log_g[:, None] * (n_valid - 1.0 - idx)[None, :])[..., None]
    dc = jnp.exp(log_g * n_valid).reshape(h, 1, 1)
    return din, dq, dk, dc


def _rotary_tables(pos, half):
    inv = ROPE_BASE ** (-jnp.arange(half, dtype=F32) / half)
    ang = pos.astype(F32)[:, None] * inv[None, :]
    return jnp.cos(ang), jnp.sin(ang)


def _retention(proj, state0, *, batch, seq, chunk, n_valid, pos0):
    h = RET_HEADS
    hb = RET_HEADS_PER_STEP
    ng = h // hb
    width = proj.shape[1]
    dk_dim = width // (6 * h)
    dv_dim = 2 * dk_dim
    n_chunks = seq // chunk
    cos, sin = _rotary_tables(pos0 + jnp.arange(seq, dtype=jnp.int32), dk_dim // 2)
    din, dq, dk, dc = _retention_tables(chunk, n_valid)
    has_state0 = state0 is not None

    row = lambda b, hg, c: b * n_chunks + c
    per_group = lambda b, hg, c: (hg, 0, 0)
    in_specs = [
        pl.BlockSpec((chunk, hb * dk_dim), lambda b, hg, c: (row(b, hg, c), hg)),
        pl.BlockSpec((chunk, hb * dk_dim), lambda b, hg, c: (row(b, hg, c), ng + hg)),
        pl.BlockSpec((chunk, hb * dv_dim), lambda b, hg, c: (row(b, hg, c), ng + hg)),
        pl.BlockSpec((chunk, hb * dv_dim), lambda b, hg, c: (row(b, hg, c), 2 * ng + hg)),
        pl.BlockSpec((chunk, dk_dim // 2), lambda b, hg, c: (c, 0)),
        pl.BlockSpec((chunk, dk_dim // 2), lambda b, hg, c: (c, 0)),
        pl.BlockSpec((hb, chunk, chunk), per_group),
        pl.BlockSpec((hb, chunk, 1), per_group),
        pl.BlockSpec((hb, chunk, 1), per_group),
        pl.BlockSpec((hb, 1, 1), per_group),
    ]
    args = [proj, proj, proj, proj, cos, sin, din, dq, dk, dc]
    state_spec = pl.BlockSpec((1, hb, dk_dim, dv_dim), lambda b, hg, c: (b, hg, 0, 0))
    if has_state0:
        in_specs.insert(0, state_spec)
        args.insert(0, state0)
    o, s_out = pl.pallas_call(
        functools.partial(_retention_kernel, n_chunks=n_chunks, has_state0=has_state0,
                          q_scale=dk_dim ** -0.5, hb=hb, dk_dim=dk_dim, dv_dim=dv_dim),
        grid=(batch, ng, n_chunks),
        in_specs=in_specs,
        out_specs=[pl.BlockSpec((chunk, hb * dv_dim), lambda b, hg, c: (row(b, hg, c), hg)), state_spec],
        out_shape=[jax.ShapeDtypeStruct((batch * seq, h * dv_dim), BF16),
                   jax.ShapeDtypeStruct((batch, h, dk_dim, dv_dim), F32)],
        scratch_shapes=[pltpu.VMEM((hb, dk_dim, dv_dim), F32)],
        compiler_params=_params(("parallel", "parallel", "arbitrary")),
        name="retention",
    )(*args)
    return o, s_out


def _softplus2(z2):
    return jnp.maximum(z2, 0.0) + jnp.log2(1.0 + jnp.exp2(-jnp.abs(z2)))


_NT = (((1,), (1,)), ((), ()))


def _sb_prompt_kernel(q_ref, k_ref, v_ref, bias_ref, o_ref, *, tq, tk):
    qi = pl.program_id(2)
    dh = q_ref.shape[-1]
    nsub = tq // tk
    q = q_ref[...]
    bias2 = bias_ref[0] * LOG2_E

    r = lax.broadcasted_iota(jnp.int32, (tk, tk), 0)
    c = lax.broadcasted_iota(jnp.int32, (tk, tk), 1)
    suffix = jnp.where(r > c, 1.0, 0.0).astype(BF16)
    causal = c < r

    def run(groups):
        raws = [[lax.dot_general(qr, k_ref[pl.ds(st, n * tk), :], _NT, preferred_element_type=F32)
                 for st, n in segs] for qr, segs, _, _, _ in groups]
        tiles = []
        for raw_segs, (_, segs, diagonal, _, _) in zip(raws, groups):
            group_tiles = []
            for raw, (_, n) in zip(raw_segs, segs):
                for s in reversed(range(n)):
                    z2 = raw[:, s * tk:(s + 1) * tk] + bias2
                    sp2 = _softplus2(z2)
                    masked = diagonal and not group_tiles
                    if masked:
                        sp2 = jnp.where(causal, sp2, 0.0)
                    group_tiles.append((z2 - sp2, sp2.astype(BF16), jnp.sum(sp2, axis=-1, keepdims=True), masked))
            tiles.append(group_tiles)
        sums = [[jnp.dot(t[1], suffix, preferred_element_type=F32) for t in group_tiles] for group_tiles in tiles]
        out = []
        for group_tiles, group_sums, (_, segs, _, carry, acc) in zip(tiles, sums, groups):
            ws = []
            for (logit, _, total, masked), in_tile in zip(group_tiles, group_sums):
                w = jnp.exp2(logit - (in_tile + carry))
                if masked:
                    w = jnp.where(causal, w, 0.0)
                ws.append(w.astype(BF16))
                carry = carry + total
            out.append((carry, ws, acc, segs))
        results = []
        for carry, ws, acc, segs in out:
            done = 0
            for st, n in segs:
                w_seg = jnp.concatenate(ws[done:done + n][::-1], axis=-1)
                v_seg = v_ref[pl.ds(st, n * tk), :]
                done += n
                rows = w_seg.shape[0]
                if rows > tk:
                    acc = acc + jnp.concatenate(
                        [jnp.dot(w_seg[:rows // 2], v_seg, preferred_element_type=F32),
                         jnp.dot(w_seg[rows // 2:], v_seg, preferred_element_type=F32)], axis=0)
                else:
                    acc = acc + jnp.dot(w_seg, v_seg, preferred_element_type=F32)
            results.append((carry, acc))
        return results

    q_groups = [q[g * tk:(g + 1) * tk] for g in range(nsub)]
    chunk_start = lambda j: pl.multiple_of(j * tq, tq)
    state = tuple(run([(q_groups[g], [(chunk_start(qi), g + 1)], True,
                        jnp.zeros((tk, 1), F32), jnp.zeros((tk, dh), F32)) for g in range(nsub)]))

    def chunks(js, st):
        segs = [(chunk_start(j), nsub) for j in js]
        return tuple(run([(q_groups[g], segs, False, st[g][0], st[g][1]) for g in range(nsub)]))

    odd = qi % 2
    state = lax.fori_loop(0, odd, lambda it, st: chunks([qi - 1], st), state)
    first = qi - 1 - odd
    state = lax.fori_loop(0, qi // 2, lambda it, st: chunks([first - 2 * it, first - 2 * it - 1], st), state)
    for g in range(nsub):
        o_ref[g * tk:(g + 1) * tk, :] = state[g][1].astype(o_ref.dtype)


def _sb_prompt(q, k, v, bias, *, batch, seq, tq=512, tk=256):
    h = SB_HEADS
    dh = q.shape[1] // h
    nq = seq // tq
    return pl.pallas_call(
        functools.partial(_sb_prompt_kernel, tq=tq, tk=tk),
        grid=(batch, h, nq),
        in_specs=[pl.BlockSpec((tq, dh), lambda b, hh, i: (b * nq + i, hh)),
                  pl.BlockSpec((seq, dh), lambda b, hh, i: (b, hh)),
                  pl.BlockSpec((seq, dh), lambda b, hh, i: (b, hh)),
                  pl.BlockSpec((1, 1, 1), lambda b, hh, i: (hh, 0, 0))],
        out_specs=pl.BlockSpec((tq, dh), lambda b, hh, i: (b * nq + i, hh)),
        out_shape=jax.ShapeDtypeStruct(q.shape, BF16),
        compiler_params=_params(("parallel", "parallel", "parallel")),
        name="sb_prompt",
    )(q, k, v, bias.reshape(h, 1, 1))


def _sb_decode_kernel(pt_ref, q_ref, bias_ref, kn_ref, vn_ref, *refs, pages_per_step, n_steps, page, heads,
                      t_new):
    del pt_ref
    k_refs = refs[:pages_per_step]
    v_refs = refs[pages_per_step:2 * pages_per_step]
    o_ref, acc_ref, carry_ref = refs[2 * pages_per_step:]
    p = pl.program_id(1)
    hh = SUBLANES_F32
    n_groups = heads // hh
    rows = hh * t_new
    lanes = page * hh
    n_tiles = lanes // LANES
    dh = q_ref.shape[-1]

    rr = lax.broadcasted_iota(jnp.int32, (LANES, 2 * LANES), 0)
    cc = lax.broadcasted_iota(jnp.int32, (LANES, 2 * LANES), 1)
    suffix = jnp.where((cc >= LANES) | (rr // hh > cc // hh), 1.0, 0.0).astype(BF16)
    lane = lax.broadcasted_iota(jnp.int32, (rows, lanes), 1)
    row = lax.broadcasted_iota(jnp.int32, (rows, lanes), 0)
    own = lane % hh == row // t_new

    def rows_of(ref, grp):
        return ref[0, :, pl.ds(grp * hh, hh), :].reshape(lanes, dh).astype(BF16)

    def update(page_refs, masked):
        keep = own
        if masked:
            keep = own & (lane // hh < row % t_new)
        work = [(k_ref, v_ref, grp) for grp in range(n_groups) for k_ref, v_ref in page_refs]
        z2s = [lax.dot_general(q_ref[0, grp], rows_of(k_ref, grp), _NT, preferred_element_type=F32)
               + bias_ref[grp] * LOG2_E for k_ref, _, grp in work]
        sp2s = [jnp.where(keep, _softplus2(z2), 0.0) for z2 in z2s]
        sums = []
        for sp2 in sp2s:
            stacked = jnp.concatenate([sp2[:, j * LANES:(j + 1) * LANES] for j in range(n_tiles)], axis=0)
            sums.append(jnp.dot(stacked.astype(BF16), suffix, preferred_element_type=F32))
        ws = []
        run = None
        for i, (_, _, grp) in enumerate(work):
            if i % len(page_refs) == 0:
                run = carry_ref[grp]
            later = [None] * n_tiles
            for j in reversed(range(n_tiles)):
                tile_sums = sums[i][j * rows:(j + 1) * rows]
                later[j] = tile_sums[:, :LANES] + run
                run = run + tile_sums[:, LANES:]
            if (i + 1) % len(page_refs) == 0:
                carry_ref[grp] = run
            w = jnp.exp2(z2s[i] - sp2s[i] - jnp.concatenate(later, axis=-1))
            ws.append(jnp.where(keep, w, 0.0).astype(BF16))
        for grp in range(n_groups):
            acc = acc_ref[grp]
            for i, (_, v_ref, g2) in enumerate(work):
                if g2 == grp:
                    acc = acc + jnp.dot(ws[i], rows_of(v_ref, grp), preferred_element_type=F32)
            acc_ref[grp] = acc

    @pl.when(p == 0)
    def _():
        acc_ref[...] = jnp.zeros_like(acc_ref)
        carry_ref[...] = jnp.zeros_like(carry_ref)
        update([(kn_ref, vn_ref)], True)

    update(list(zip(k_refs, v_refs)), False)

    @pl.when(p == n_steps - 1)
    def _():
        o_ref[0] = acc_ref[...]


def _sb_decode(q, k_new, v_new, cache_k, cache_v, page_table, bias, *, batch, t_new, pages_per_step=8):
    h = SB_HEADS
    hh = SUBLANES_F32
    n_pool, page, _, dh = cache_k.shape
    assert h % hh == 0 and (page * hh) % LANES == 0 and LANES % hh == 0
    n_groups = h // hh
    rows = hh * t_new
    n_pages = page_table.shape[1]
    n_steps = n_pages // pages_per_step
    q_rows = jnp.transpose(q.reshape(batch, t_new, n_groups, hh, dh), (0, 2, 3, 1, 4))
    q_rows = q_rows.reshape(batch, n_groups, rows, dh).astype(BF16)
    bias_rows = jnp.repeat(bias.astype(F32).reshape(n_groups, hh), t_new, axis=1)[..., None]
    pad = lambda a: jnp.pad(a.reshape(batch, t_new, h, dh), ((0, 0), (0, page - t_new), (0, 0), (0, 0)))
    kn, vn = pad(k_new), pad(v_new)

    def page_map(g):
        return lambda b, p, pt: (pt[b, n_pages - 1 - (p * pages_per_step + g)], 0, 0, 0)

    page_specs = [pl.BlockSpec((1, page, h, dh), page_map(g)) for g in range(pages_per_step)]
    per_batch = lambda b, p, pt: (b, 0, 0, 0)
    acc = pl.pallas_call(
        functools.partial(_sb_decode_kernel, pages_per_step=pages_per_step, n_steps=n_steps, page=page, heads=h,
                          t_new=t_new),
        grid_spec=pltpu.PrefetchScalarGridSpec(
            num_scalar_prefetch=1,
            grid=(batch, n_steps),
            in_specs=[pl.BlockSpec((1, n_groups, rows, dh), per_batch),
                      pl.BlockSpec((n_groups, rows, 1), lambda b, p, pt: (0, 0, 0)),
                      pl.BlockSpec((1, page, h, dh), per_batch),
                      pl.BlockSpec((1, page, h, dh), per_batch)] + page_specs + page_specs,
            out_specs=pl.BlockSpec((1, n_groups, rows, dh), per_batch),
            scratch_shapes=[pltpu.VMEM((n_groups, rows, dh), F32), pltpu.VMEM((n_groups, rows, LANES), F32)],
        ),
        out_shape=jax.ShapeDtypeStruct((batch, n_groups, rows, dh), F32),
        compiler_params=_params(("parallel", "arbitrary")),
        name="sb_decode",
    )(page_table, q_rows, bias_rows, kn, vn, *([cache_k] * pages_per_step), *([cache_v] * pages_per_step))
    o = jnp.transpose(acc.reshape(batch, n_groups, hh, t_new, dh), (0, 3, 1, 2, 4))
    return o.reshape(batch * t_new, h * dh).astype(BF16)


def _trunk(x, w, norms, sb_bias, *, batch, seq, pos0, state0, decode, emit):
    ln_g, ln_b = norms
    m, d = x.shape
    g = lambda layer, i: ln_g[layer, i][None, :]
    b = lambda layer, i: ln_b[layer, i][None, :]
    copies = {}

    def keep(name, outs):
        if emit:
            copies[name] = _Weight(outs[-1])
            return outs[:-1]
        return outs

    proj, = keep("ret_in", _proj(x, w["ret_in"], [F32], emit=emit))
    if seq % RET_CHUNK == 0:
        o, s_new = _retention(proj, state0, batch=batch, seq=seq, chunk=RET_CHUNK, n_valid=RET_CHUNK, pos0=pos0)
    else:
        chunk = SUBLANES_F32
        padded = jnp.pad(proj.reshape(batch, seq, -1), ((0, 0), (0, chunk - seq), (0, 0)))
        o, s_new = _retention(padded.reshape(batch * chunk, -1), state0, batch=batch, seq=chunk, chunk=chunk,
                              n_valid=seq, pos0=pos0)
        o = o.reshape(batch, chunk, -1)[:, :seq].reshape(m, -1)
    x, xb = keep("ret_out", _proj_norm(o, w["ret_out"], x, g(0, 0), b(0, 0), emit=emit))
    outs = _mlp(x, xb, w["ff1_0"], w["ff2_0"], g(0, 1), b(0, 1), emit=emit)
    if emit:
        copies["ff1_0"], copies["ff2_0"] = _Weight(outs[2]), _Weight(outs[3])
    x, xb = outs[:2]

    q_scale = LOG2_E * (d // SB_HEADS) ** -0.5
    if decode is None:
        k_new, kb, v_new, vb, q = _proj_kvq(xb, w["k"], w["v"], w["sb_q"], q_scale)
        o = _sb_prompt(q, kb, vb, sb_bias, batch=batch, seq=seq)
    else:
        cache_k, cache_v, page_table = decode
        k_new, = keep("k", _proj(xb, w["k"], [F32], emit=emit))
        v_new, = keep("v", _proj(xb, w["v"], [F32], emit=emit))
        q, = keep("sb_q", _proj(xb, w["sb_q"], [F32], emit=emit, scale=q_scale))
        o = _sb_decode(q, k_new, v_new, cache_k, cache_v, page_table, sb_bias, batch=batch, t_new=seq)
    x, xb = keep("sb_o", _proj_norm(o, w["sb_o"], x, g(1, 0), b(1, 0), emit=emit))
    outs = _mlp(x, xb, w["ff1_1"], w["ff2_1"], g(1, 1), b(1, 1), emit=emit)
    if emit:
        copies["ff1_1"], copies["ff2_1"] = _Weight(outs[2]), _Weight(outs[3])
    return outs[0], s_new, k_new, v_new, copies


def kernel(x_prompt, x_sample, state_ret, cache_k, cache_v, page_table, w_ret_in, w_ret_out, w_kv, w_sb_q,
           w_sb_o, sb_bias, w_ff1, w_ff2, ln_g, ln_b):
    bp, tp, d = x_prompt.shape
    bs, ts, _ = x_sample.shape
    dh = d // SB_HEADS
    n_pages = page_table.shape[1]
    page = cache_k.shape[1]
    weights = {"ret_in": _Weight(w_ret_in, 0), "ret_out": _Weight(w_ret_out, 0),
               "k": _Weight(w_kv, col0=0, ncols=d), "v": _Weight(w_kv, col0=d, ncols=d),
               "sb_q": _Weight(w_sb_q, 0), "sb_o": _Weight(w_sb_o, 0),
               "ff1_0": _Weight(w_ff1, 0), "ff2_0": _Weight(w_ff2, 0),
               "ff1_1": _Weight(w_ff1, 1), "ff2_1": _Weight(w_ff2, 1)}
    norms = (ln_g.astype(F32), ln_b.astype(F32))

    y_s, s_s, k_s, v_s, weights_bf16 = _trunk(
        x_sample.reshape(bs * ts, d), weights, norms, sb_bias[0], batch=bs, seq=ts, pos0=n_pages * page,
        state0=state_ret[0].astype(F32), decode=(cache_k, cache_v, page_table), emit=True)
    y_p, s_p, k_p, v_p, _ = _trunk(
        x_prompt.reshape(bp * tp, d), weights_bf16, norms, sb_bias[0], batch=bp, seq=tp, pos0=0,
        state0=None, decode=None, emit=False)
    heads = lambda a, bb, tt: a.reshape(bb, tt, SB_HEADS, dh)
    return (y_p.reshape(bp, tp, d), y_s.reshape(bs, ts, d),
            s_p[None].astype(state_ret.dtype), heads(k_p, bp, tp), heads(v_p, bp, tp),
            s_s[None].astype(state_ret.dtype), heads(k_s, bs, ts), heads(v_s, bs, ts))
```

```python
import functools

import jax
import jax.numpy as jnp
from jax import lax
from jax.experimental import pallas as pl
from jax.experimental.pallas import tpu as pltpu

F32 = jnp.float32
BF16 = jnp.bfloat16

RET_HEADS = 8
RET_CHUNK = 128
ROPE_BASE = 10000.0
SB_HEADS = 16
LN_EPS = 1e-5
GN_EPS = 1e-6
DEPTH = 2
ALPHA = (2.0 * DEPTH) ** 0.25

V7X_VMEM_BYTES = 64 * 1024 * 1024
VMEM_LIMIT_BYTES = V7X_VMEM_BYTES - 12 * 1024 * 1024
LANES = 128
SUBLANES_F32 = 8
LOG2_E = 1.4426950408889634


def _params(semantics):
    return pltpu.CompilerParams(dimension_semantics=semantics, vmem_limit_bytes=VMEM_LIMIT_BYTES)


def _row_tile(m, want):
    return want if m % want == 0 else m


class _Weight:
    def __init__(self, array, layer=0, col0=0, ncols=None):
        self.array, self.layer, self.col0 = array, layer, col0
        self.rows = array.shape[-2]
        self.ncols = array.shape[-1] - col0 if ncols is None else ncols
        self.is_bf16 = array.dtype == BF16

    def spec(self, block, index_map):
        off = self.col0 // block[1]
        assert self.col0 % block[1] == 0
        if self.array.ndim == 2:
            return pl.BlockSpec(block, lambda *ids: (index_map(*ids)[0], index_map(*ids)[1] + off))
        return pl.BlockSpec((pl.Squeezed(),) + block,
                            lambda *ids: (self.layer, index_map(*ids)[0], index_map(*ids)[1] + off))


def _bf16(x):
    return x if x.dtype == BF16 else x.astype(BF16)


def _proj_kernel(x_ref, w_ref, *refs, n_out, cast_x, emit, scale, bf16_tiles):
    outs = refs[:n_out]
    rest = refs[n_out:]
    if cast_x:
        xb_ref = rest[-1]

        @pl.when(pl.program_id(1) == 0)
        def _():
            xb_ref[...] = x_ref[...].astype(BF16)

        xb = xb_ref[...]
    else:
        xb = x_ref[...]
    w = _bf16(w_ref[...])
    if emit:
        rest[0][...] = w
    acc = jnp.dot(xb, w, preferred_element_type=F32)
    if scale is not None:
        acc = acc * scale
    if bf16_tiles is None:
        for o in outs:
            o[...] = acc.astype(o.dtype)
        return
    b0, b1 = bf16_tiles
    j = pl.program_id(1)
    in_bf16 = (j >= b0) & (j < b1)

    @pl.when(in_bf16)
    def _():
        outs[1][...] = acc.astype(BF16)

    @pl.when(jnp.logical_not(in_bf16))
    def _():
        outs[0][...] = acc


def _proj(x, w, out_dtypes, *, emit=False, scale=None, bf16_cols=None, tm=1024, tn=1024):
    m, k = x.shape
    n = w.ncols
    tm = _row_tile(m, tm)
    tn = _row_tile(n, tn)
    assert not emit or m == tm
    cast_x = x.dtype != BF16
    scratch = [pltpu.VMEM((tm, k), BF16)] if cast_x else []
    emit_specs = [pl.BlockSpec((k, tn), lambda i, j: (0, j))] if emit else []
    emit_shapes = [jax.ShapeDtypeStruct((k, n), BF16)] if emit else []
    if bf16_cols is None:
        bf16_tiles = None
        out_specs = [pl.BlockSpec((tm, tn), lambda i, j: (i, j)) for _ in out_dtypes]
        out_shapes = [jax.ShapeDtypeStruct((m, n), dt) for dt in out_dtypes]
    else:
        c0, c1 = bf16_cols
        assert c0 % tn == 0 and c1 % tn == 0 and 0 < c0 < c1
        b0, b1 = c0 // tn, c1 // tn
        bf16_tiles = (b0, b1)
        f32_tile = lambda i, j: (i, jnp.where(j < b0, j, jnp.where(j >= b1, j - (b1 - b0), b0 - 1)))
        bf16_tile = lambda i, j: (i, jnp.clip(j - b0, 0, b1 - b0 - 1))
        out_specs = [pl.BlockSpec((tm, tn), f32_tile), pl.BlockSpec((tm, tn), bf16_tile)]
        out_shapes = [jax.ShapeDtypeStruct((m, n - (c1 - c0)), F32), jax.ShapeDtypeStruct((m, c1 - c0), BF16)]
        out_dtypes = [F32, BF16]
    return pl.pallas_call(
        functools.partial(_proj_kernel, n_out=len(out_dtypes), cast_x=cast_x, emit=emit, scale=scale,
                          bf16_tiles=bf16_tiles),
        grid=(m // tm, n // tn),
        in_specs=[pl.BlockSpec((tm, k), lambda i, j: (i, 0)),
                  w.spec((k, tn), lambda i, j: (0, j))],
        out_specs=out_specs + emit_specs,
        out_shape=out_shapes + emit_shapes,
        scratch_shapes=scratch,
        compiler_params=_params(("parallel", "arbitrary")),
        name="proj",
    )(x, w.array)


def _proj_kvq_kernel(x_ref, wk_ref, wv_ref, wq_ref, after_ref, kf_ref, vf_ref, q_ref, *, tiles, q_scale):
    del after_ref
    j = pl.program_id(1)
    x = x_ref[...]

    @pl.when(j < tiles)
    def _():
        kf_ref[...] = jnp.dot(x, wk_ref[...], preferred_element_type=F32)

    @pl.when((j >= tiles) & (j < 2 * tiles))
    def _():
        vf_ref[...] = jnp.dot(x, wv_ref[...], preferred_element_type=F32)

    @pl.when(j >= 2 * tiles)
    def _():
        q_ref[...] = (jnp.dot(x, wq_ref[...], preferred_element_type=F32) * q_scale).astype(BF16)


def _proj_kvq(xb, wk, wv, wq, q_scale, after, *, tm=1024, tn=512):
    m, k = xb.shape
    n = wk.ncols
    assert wk.is_bf16 and wv.is_bf16 and wq.is_bf16 and wv.ncols == n and wq.ncols == n
    tiles = n // tn

    def tile_of(which):
        return lambda i, j: (i, jnp.clip(j - which * tiles, 0, tiles - 1))

    w_specs = [w.spec((k, tn), lambda i, j, t=tile_of(which): (0, t(i, j)[1]))
               for which, w in enumerate((wk, wv, wq))]
    out_spec = lambda which: pl.BlockSpec((tm, tn), tile_of(which))
    shape = lambda dt: jax.ShapeDtypeStruct((m, n), dt)
    return pl.pallas_call(
        functools.partial(_proj_kvq_kernel, tiles=tiles, q_scale=q_scale),
        grid=(m // tm, 3 * tiles),
        in_specs=[pl.BlockSpec((tm, k), lambda i, j: (i, 0))] + w_specs + [pl.BlockSpec((1, 1), lambda i, j: (0, 0))],
        out_specs=[out_spec(0), out_spec(1), out_spec(2)],
        out_shape=[shape(F32), shape(F32), shape(BF16)],
        compiler_params=_params(("parallel", "arbitrary")),
        name="proj_kvq",
    )(xb, wk.array, wv.array, wq.array, after)


def _post_norm_store(res, sub, g_ref, b_ref, of_ref, ob_ref):
    y = ALPHA * res + sub
    mu = jnp.mean(y, axis=-1, keepdims=True)
    yc = y - mu
    var = jnp.mean(yc * yc, axis=-1, keepdims=True)
    out = yc * lax.rsqrt(var + LN_EPS) * g_ref[...] + b_ref[...]
    of_ref[...] = out
    ob_ref[...] = out.astype(BF16)


ACC_COLS = 512


def _zero_on_first(acc_ref, step):
    @pl.when(step == 0)
    def _():
        acc_ref[...] = jnp.zeros_like(acc_ref)


def _accumulate_dot(acc_ref, lhs, w_ref, wq_ref=None):
    n = acc_ref.shape[1]
    tn = ACC_COLS if n % ACC_COLS == 0 else n
    for j in range(n // tn):
        cols = slice(j * tn, (j + 1) * tn)
        w = _bf16(w_ref[:, cols])
        if wq_ref is not None:
            wq_ref[:, cols] = w
        acc_ref[:, cols] += jnp.dot(lhs, w, preferred_element_type=F32)


def _proj_norm_kernel(x_ref, w_ref, res_ref, g_ref, b_ref, of_ref, ob_ref, *rest, nk, emit):
    wq_ref = rest[0] if emit else None
    if nk == 1:
        w = _bf16(w_ref[...])
        if emit:
            wq_ref[...] = w
        sub = jnp.dot(x_ref[...], w, preferred_element_type=F32)
        _post_norm_store(res_ref[...], sub, g_ref, b_ref, of_ref, ob_ref)
        return
    acc_ref = rest[-1]
    k = pl.program_id(1)
    _zero_on_first(acc_ref, k)
    _accumulate_dot(acc_ref, x_ref[...], w_ref, wq_ref)

    @pl.when(k == nk - 1)
    def _():
        _post_norm_store(res_ref[...], acc_ref[...], g_ref, b_ref, of_ref, ob_ref)


def _proj_norm(x, w, res, g, b, *, emit=False, tm=512):
    m, kdim = x.shape
    n = w.ncols
    tm = _row_tile(m, tm)
    assert not emit or m == tm
    tk = kdim if w.is_bf16 else min(kdim, 1024)
    nk = kdim // tk
    row = lambda i, k: (i, 0)
    emit_specs = [pl.BlockSpec((tk, n), lambda i, k: (k, 0))] if emit else []
    emit_shapes = [jax.ShapeDtypeStruct((kdim, n), BF16)] if emit else []
    w_spec = w.spec((tk, n), lambda i, k: (k, 0))
    if nk == 1:
        w_spec = pl.BlockSpec(w_spec.block_shape, w_spec.index_map, pipeline_mode=pl.Buffered(1))
    return pl.pallas_call(
        functools.partial(_proj_norm_kernel, nk=nk, emit=emit),
        grid=(m // tm, nk),
        in_specs=[pl.BlockSpec((tm, tk), lambda i, k: (i, k)),
                  w_spec,
                  pl.BlockSpec((tm, n), row),
                  pl.BlockSpec((1, n), lambda i, k: (0, 0)),
                  pl.BlockSpec((1, n), lambda i, k: (0, 0))],
        out_specs=[pl.BlockSpec((tm, n), row), pl.BlockSpec((tm, n), row)] + emit_specs,
        out_shape=[jax.ShapeDtypeStruct((m, n), F32), jax.ShapeDtypeStruct((m, n), BF16)] + emit_shapes,
        scratch_shapes=[pltpu.VMEM((tm, n), F32)] if nk > 1 else [],
        compiler_params=_params(("parallel", "arbitrary")),
        name="proj_norm",
    )(x, w.array, res, g, b)


def _mlp_kernel(x_ref, xb_ref, w1_ref, w2_ref, g_ref, b_ref, of_ref, ob_ref, *rest, nf, emit):
    acc_ref = rest[-1]
    f = pl.program_id(1)
    _zero_on_first(acc_ref, f)
    w1 = _bf16(w1_ref[...])
    if emit:
        rest[0][...] = w1
    h = jnp.dot(xb_ref[...], w1, preferred_element_type=F32)
    h = jnp.square(jnp.maximum(h, 0.0)).astype(BF16)
    _accumulate_dot(acc_ref, h, w2_ref, rest[1] if emit else None)

    @pl.when(f == nf - 1)
    def _():
        _post_norm_store(x_ref[...], acc_ref[...], g_ref, b_ref, of_ref, ob_ref)


def _mlp(x, xb, w1, w2, g, b, *, emit=False, tm=512):
    m, d = x.shape
    dff = w1.ncols
    tm = _row_tile(m, tm)
    assert not emit or m == tm
    tf = 1024 if w1.is_bf16 else 512
    nf = dff // tf
    row = lambda i, f: (i, 0)
    emit_specs = [pl.BlockSpec((d, tf), lambda i, f: (0, f)), pl.BlockSpec((tf, d), lambda i, f: (f, 0))] if emit else []
    emit_shapes = [jax.ShapeDtypeStruct((d, dff), BF16), jax.ShapeDtypeStruct((dff, d), BF16)] if emit else []
    return pl.pallas_call(
        functools.partial(_mlp_kernel, nf=nf, emit=emit),
        grid=(m // tm, nf),
        in_specs=[pl.BlockSpec((tm, d), row),
                  pl.BlockSpec((tm, d), row),
                  w1.spec((d, tf), lambda i, f: (0, f)),
                  w2.spec((tf, d), lambda i, f: (f, 0)),
                  pl.BlockSpec((1, d), lambda i, f: (0, 0)),
                  pl.BlockSpec((1, d), lambda i, f: (0, 0))],
        out_specs=[pl.BlockSpec((tm, d), row), pl.BlockSpec((tm, d), row)] + emit_specs,
        out_shape=[jax.ShapeDtypeStruct((m, d), F32), jax.ShapeDtypeStruct((m, d), BF16)] + emit_shapes,
        scratch_shapes=[pltpu.VMEM((tm, d), F32)],
        compiler_params=_params(("parallel", "arbitrary")),
        name="mlp",
    )(x, xb, w1.array, w2.array, g, b)


RET_HEADS_PER_STEP = 8


def _rotary(x, cos, sin):
    half = x.shape[-1] // 2
    x1, x2 = x[:, :half], x[:, half:]
    return jnp.concatenate([x1 * cos - x2 * sin, x1 * sin + x2 * cos], axis=-1)


def _retention_kernel(*refs, n_chunks, has_state0, q_scale, hb, dk_dim, dv_dim):
    if has_state0:
        s0_ref, refs = refs[0], refs[1:]
    (q_ref, k_ref, v_ref, g_ref, cos_ref, sin_ref, din_ref, dq_ref, dk_ref, dc_ref,
     o_ref, sout_ref, s_ref) = refs
    c = pl.program_id(2)

    @pl.when(c == 0)
    def _():
        if has_state0:
            s_ref[...] = s0_ref[0]
        else:
            s_ref[...] = jnp.zeros_like(s_ref)

    cos, sin = cos_ref[...], sin_ref[...]
    for i in range(hb):
        kcols = slice(i * dk_dim, (i + 1) * dk_dim)
        vcols = slice(i * dv_dim, (i + 1) * dv_dim)
        q = (_rotary(q_ref[:, kcols], cos, sin) * q_scale).astype(BF16)
        k = _rotary(k_ref[:, kcols], cos, sin)
        v = v_ref[:, vcols]
        state = s_ref[i]

        scores = lax.dot_general(q, k.astype(BF16), (((1,), (1,)), ((), ())),
                                 preferred_element_type=F32) * din_ref[i]
        inner = jnp.dot(scores.astype(BF16), v, preferred_element_type=F32)
        cross = jnp.dot(q, state.astype(BF16), preferred_element_type=F32) * dq_ref[i]
        kd = (k * dk_ref[i]).astype(BF16)
        s_ref[i] = dc_ref[i] * state + lax.dot_general(kd, v, (((0,), (0,)), ((), ())),
                                                       preferred_element_type=F32)

        o = inner + cross
        mu = jnp.mean(o, axis=-1, keepdims=True)
        oc = o - mu
        var = jnp.mean(oc * oc, axis=-1, keepdims=True)
        gate = g_ref[:, vcols]
        gate = gate / (1.0 + jnp.exp(-gate))
        o_ref[:, vcols] = (gate * (oc * lax.rsqrt(var + GN_EPS))).astype(o_ref.dtype)

    @pl.when(c == n_chunks - 1)
    def _():
        sout_ref[0] = s_ref[...]


def _retention_tables(chunk, n_valid):
    h = RET_HEADS
    log_g = jnp.log1p(-jnp.power(2.0, -5.0 - jnp.arange(h, dtype=F32)))
    idx = jnp.arange(chunk, dtype=F32)
    diff = idx[:, None] - idx[None, :]
    din = jnp.where(diff[None] >= 0, jnp.exp(log_g[:, None, None] * jnp.maximum(diff, 0.0)[None]), 0.0)
    dq = jnp.exp(log_g[:, None] * (idx + 1.0)[None, :])[..., None]
    dk = jnp.exp(log_g[:, None] * (n_valid - 1.0 - idx)[None, :])[..., None]
    dc = jnp.exp(log_g * n_valid).reshape(h, 1, 1)
    return din, dq, dk, dc


def _rotary_tables(pos, half):
    inv = ROPE_BASE ** (-jnp.arange(half, dtype=F32) / half)
    ang = pos.astype(F32)[:, None] * inv[None, :]
    return jnp.cos(ang), jnp.sin(ang)


def _retention(qkg, v, state0, *, batch, seq, chunk, n_valid, pos0):
    h = RET_HEADS
    hb = RET_HEADS_PER_STEP
    ng = h // hb
    dv_dim = v.shape[1] // h
    dk_dim = dv_dim // 2
    assert qkg.shape[1] == 2 * h * dk_dim + h * dv_dim
    n_chunks = seq // chunk
    cos, sin = _rotary_tables(pos0 + jnp.arange(seq, dtype=jnp.int32), dk_dim // 2)
    din, dq, dk, dc = _retention_tables(chunk, n_valid)
    has_state0 = state0 is not None

    row = lambda b, hg, c: b * n_chunks + c
    per_group = lambda b, hg, c: (hg, 0, 0)
    in_specs = [
        pl.BlockSpec((chunk, hb * dk_dim), lambda b, hg, c: (row(b, hg, c), hg)),
        pl.BlockSpec((chunk, hb * dk_dim), lambda b, hg, c: (row(b, hg, c), ng + hg)),
        pl.BlockSpec((chunk, hb * dv_dim), lambda b, hg, c: (row(b, hg, c), hg)),
        pl.BlockSpec((chunk, hb * dv_dim), lambda b, hg, c: (row(b, hg, c), ng + hg)),
        pl.BlockSpec((chunk, dk_dim // 2), lambda b, hg, c: (c, 0)),
        pl.BlockSpec((chunk, dk_dim // 2), lambda b, hg, c: (c, 0)),
        pl.BlockSpec((hb, chunk, chunk), per_group),
        pl.BlockSpec((hb, chunk, 1), per_group),
        pl.BlockSpec((hb, chunk, 1), per_group),
        pl.BlockSpec((hb, 1, 1), per_group),
    ]
    args = [qkg, qkg, v, qkg, cos, sin, din, dq, dk, dc]
    state_spec = pl.BlockSpec((1, hb, dk_dim, dv_dim), lambda b, hg, c: (b, hg, 0, 0))
    if has_state0:
        in_specs.insert(0, state_spec)
        args.insert(0, state0)
    o, s_out = pl.pallas_call(
        functools.partial(_retention_kernel, n_chunks=n_chunks, has_state0=has_state0,
                          q_scale=dk_dim ** -0.5, hb=hb, dk_dim=dk_dim, dv_dim=dv_dim),
        grid=(batch, ng, n_chunks),
        in_specs=in_specs,
        out_specs=[pl.BlockSpec((chunk, hb * dv_dim), lambda b, hg, c: (row(b, hg, c), hg)), state_spec],
        out_shape=[jax.ShapeDtypeStruct((batch * seq, h * dv_dim), BF16),
                   jax.ShapeDtypeStruct((batch, h, dk_dim, dv_dim), F32)],
        scratch_shapes=[pltpu.VMEM((hb, dk_dim, dv_dim), F32)],
        compiler_params=_params(("parallel", "parallel", "arbitrary")),
        name="retention",
    )(*args)
    return o, s_out


def _softplus2(z2):
    return jnp.maximum(z2, 0.0) + jnp.log2(1.0 + jnp.exp2(-jnp.abs(z2)))


_NT = (((1,), (1,)), ((), ()))


def _sb_prompt_kernel(q_ref, k_ref, v_ref, bias_ref, o_ref, *, tq, tk):
    qi = pl.program_id(2)
    dh = q_ref.shape[-1]
    nsub = tq // tk
    q = q_ref[...]
    bias2 = bias_ref[0] * LOG2_E

    r = lax.broadcasted_iota(jnp.int32, (tk, tk), 0)
    c = lax.broadcasted_iota(jnp.int32, (tk, tk), 1)
    suffix = jnp.where(r > c, 1.0, 0.0).astype(BF16)
    causal = c < r

    def run(groups):
        raws = [[lax.dot_general(qr, k_ref[pl.ds(st, n * tk), :].astype(BF16), _NT, preferred_element_type=F32)
                 for st, n in segs] for qr, segs, _, _, _ in groups]
        tiles = []
        for raw_segs, (_, segs, diagonal, _, _) in zip(raws, groups):
            group_tiles = []
            for raw, (_, n) in zip(raw_segs, segs):
                for s in reversed(range(n)):
                    z2 = raw[:, s * tk:(s + 1) * tk] + bias2
                    sp2 = _softplus2(z2)
                    masked = diagonal and not group_tiles
                    if masked:
                        sp2 = jnp.where(causal, sp2, 0.0)
                    group_tiles.append((z2 - sp2, sp2.astype(BF16), jnp.sum(sp2, axis=-1, keepdims=True), masked))
            tiles.append(group_tiles)
        sums = [[jnp.dot(t[1], suffix, preferred_element_type=F32) for t in group_tiles] for group_tiles in tiles]
        out = []
        for group_tiles, group_sums, (_, segs, _, carry, acc) in zip(tiles, sums, groups):
            ws = []
            for (logit, _, total, masked), in_tile in zip(group_tiles, group_sums):
                w = jnp.exp2(logit - (in_tile + carry))
                if masked:
                    w = jnp.where(causal, w, 0.0)
                ws.append(w.astype(BF16))
                carry = carry + total
            out.append((carry, ws, acc, segs))
        results = []
        for carry, ws, acc, segs in out:
            done = 0
            for st, n in segs:
                w_seg = jnp.concatenate(ws[done:done + n][::-1], axis=-1)
                v_seg = v_ref[pl.ds(st, n * tk), :].astype(BF16)
                done += n
                acc = acc + jnp.dot(w_seg, v_seg, preferred_element_type=F32)
            results.append((carry, acc))
        return results

    q_groups = [q[g * tk:(g + 1) * tk] for g in range(nsub)]
    chunk_start = lambda j: pl.multiple_of(j * tq, tq)
    state = tuple(run([(q_groups[g], [(chunk_start(qi), g + 1)], True,
                        jnp.zeros((tk, 1), F32), jnp.zeros((tk, dh), F32)) for g in range(nsub)]))

    def chunks(js, st):
        segs = [(chunk_start(j), nsub) for j in js]
        return tuple(run([(q_groups[g], segs, False, st[g][0], st[g][1]) for g in range(nsub)]))

    odd = qi % 2
    state = lax.fori_loop(0, odd, lambda it, st: chunks([qi - 1], st), state)
    first = qi - 1 - odd
    state = lax.fori_loop(0, qi // 2, lambda it, st: chunks([first - 2 * it, first - 2 * it - 1], st), state)
    for g in range(nsub):
        o_ref[g * tk:(g + 1) * tk, :] = state[g][1].astype(o_ref.dtype)


def _sb_prompt(q, k, v, bias, *, batch, seq, tq=512, tk=256):
    h = SB_HEADS
    dh = q.shape[1] // h
    nq = seq // tq
    return pl.pallas_call(
        functools.partial(_sb_prompt_kernel, tq=tq, tk=tk),
        grid=(batch, h, nq),
        in_specs=[pl.BlockSpec((tq, dh), lambda b, hh, i: (b * nq + i, hh)),
                  pl.BlockSpec((seq, dh), lambda b, hh, i: (b, hh)),
                  pl.BlockSpec((seq, dh), lambda b, hh, i: (b, hh)),
                  pl.BlockSpec((1, 1, 1), lambda b, hh, i: (hh, 0, 0))],
        out_specs=pl.BlockSpec((tq, dh), lambda b, hh, i: (b * nq + i, hh)),
        out_shape=jax.ShapeDtypeStruct(q.shape, BF16),
        compiler_params=_params(("parallel", "parallel", "parallel")),
        name="sb_prompt",
    )(q, k, v, bias.reshape(h, 1, 1))


def _sb_decode_kernel(pt_ref, q_ref, bias_ref, kn_ref, vn_ref, *refs, pages_per_step, n_steps, page, heads,
                      t_new):
    del pt_ref
    k_refs = refs[:pages_per_step]
    v_refs = refs[pages_per_step:2 * pages_per_step]
    o_ref, acc_ref, carry_ref = refs[2 * pages_per_step:]
    p = pl.program_id(1)
    hh = SUBLANES_F32
    n_groups = heads // hh
    rows = hh * t_new
    lanes = page * hh
    n_tiles = lanes // LANES
    dh = q_ref.shape[-1]

    rr = lax.broadcasted_iota(jnp.int32, (LANES, 2 * LANES), 0)
    cc = lax.broadcasted_iota(jnp.int32, (LANES, 2 * LANES), 1)
    suffix = jnp.where((cc >= LANES) | (rr // hh > cc // hh), 1.0, 0.0).astype(BF16)
    lane = lax.broadcasted_iota(jnp.int32, (rows, lanes), 1)
    row = lax.broadcasted_iota(jnp.int32, (rows, lanes), 0)
    own = lane % hh == row // t_new

    def rows_of(ref, grp):
        return ref[0, :, pl.ds(grp * hh, hh), :].reshape(lanes, dh).astype(BF16)

    def update(page_refs, masked):
        keep = own
        if masked:
            keep = own & (lane // hh < row % t_new)
        work = [(k_ref, v_ref, grp) for grp in range(n_groups) for k_ref, v_ref in page_refs]
        z2s = [lax.dot_general(q_ref[0, grp], rows_of(k_ref, grp), _NT, preferred_element_type=F32)
               + bias_ref[grp] * LOG2_E for k_ref, _, grp in work]
        sp2s = [jnp.where(keep, _softplus2(z2), 0.0) for z2 in z2s]
        sums = []
        for sp2 in sp2s:
            stacked = jnp.concatenate([sp2[:, j * LANES:(j + 1) * LANES] for j in range(n_tiles)], axis=0)
            sums.append(jnp.dot(stacked.astype(BF16), suffix, preferred_element_type=F32))
        ws = []
        run = None
        for i, (_, _, grp) in enumerate(work):
            if i % len(page_refs) == 0:
                run = carry_ref[grp]
            later = [None] * n_tiles
            for j in reversed(range(n_tiles)):
                tile_sums = sums[i][j * rows:(j + 1) * rows]
                later[j] = tile_sums[:, :LANES] + run
                run = run + tile_sums[:, LANES:]
            if (i + 1) % len(page_refs) == 0:
                carry_ref[grp] = run
            w = jnp.exp2(z2s[i] - sp2s[i] - jnp.concatenate(later, axis=-1))
            ws.append(jnp.where(keep, w, 0.0).astype(BF16))
        for grp in range(n_groups):
            acc = acc_ref[grp]
            for i, (_, v_ref, g2) in enumerate(work):
                if g2 == grp:
                    acc = acc + jnp.dot(ws[i], rows_of(v_ref, grp), preferred_element_type=F32)
            acc_ref[grp] = acc

    @pl.when(p == 0)
    def _():
        acc_ref[...] = jnp.zeros_like(acc_ref)
        carry_ref[...] = jnp.zeros_like(carry_ref)
        update([(kn_ref, vn_ref)], True)

    update(list(zip(k_refs, v_refs)), False)

    @pl.when(p == n_steps - 1)
    def _():
        o_ref[0] = acc_ref[...]


def _sb_decode(q, k_new, v_new, cache_k, cache_v, page_table, bias, *, batch, t_new, pages_per_step=8):
    h = SB_HEADS
    hh = SUBLANES_F32
    n_pool, page, _, dh = cache_k.shape
    assert h % hh == 0 and (page * hh) % LANES == 0 and LANES % hh == 0
    n_groups = h // hh
    rows = hh * t_new
    n_pages = page_table.shape[1]
    n_steps = n_pages // pages_per_step
    q_rows = jnp.transpose(q.reshape(batch, t_new, n_groups, hh, dh), (0, 2, 3, 1, 4))
    q_rows = q_rows.reshape(batch, n_groups, rows, dh).astype(BF16)
    bias_rows = jnp.repeat(bias.astype(F32).reshape(n_groups, hh), t_new, axis=1)[..., None]
    pad = lambda a: jnp.pad(a.reshape(batch, t_new, h, dh), ((0, 0), (0, page - t_new), (0, 0), (0, 0)))
    kn, vn = pad(k_new), pad(v_new)

    def page_map(g):
        return lambda b, p, pt: (pt[b, n_pages - 1 - (p * pages_per_step + g)], 0, 0, 0)

    page_specs = [pl.BlockSpec((1, page, h, dh), page_map(g)) for g in range(pages_per_step)]
    per_batch = lambda b, p, pt: (b, 0, 0, 0)
    acc = pl.pallas_call(
        functools.partial(_sb_decode_kernel, pages_per_step=pages_per_step, n_steps=n_steps, page=page, heads=h,
                          t_new=t_new),
        grid_spec=pltpu.PrefetchScalarGridSpec(
            num_scalar_prefetch=1,
            grid=(batch, n_steps),
            in_specs=[pl.BlockSpec((1, n_groups, rows, dh), per_batch),
                      pl.BlockSpec((n_groups, rows, 1), lambda b, p, pt: (0, 0, 0)),
                      pl.BlockSpec((1, page, h, dh), per_batch),
                      pl.BlockSpec((1, page, h, dh), per_batch)] + page_specs + page_specs,
            out_specs=pl.BlockSpec((1, n_groups, rows, dh), per_batch),
            scratch_shapes=[pltpu.VMEM((n_groups, rows, dh), F32), pltpu.VMEM((n_groups, rows, LANES), F32)],
        ),
        out_shape=jax.ShapeDtypeStruct((batch, n_groups, rows, dh), F32),
        compiler_params=_params(("parallel", "arbitrary")),
        name="sb_decode",
    )(page_table, q_rows, bias_rows, kn, vn, *([cache_k] * pages_per_step), *([cache_v] * pages_per_step))
    o = jnp.transpose(acc.reshape(batch, n_groups, hh, t_new, dh), (0, 3, 1, 2, 4))
    return o.reshape(batch * t_new, h * dh).astype(BF16)


def _trunk(x, w, norms, sb_bias, *, batch, seq, pos0, state0, decode, emit, after=None):
    ln_g, ln_b = norms
    m, d = x.shape
    g = lambda layer, i: ln_g[layer, i][None, :]
    b = lambda layer, i: ln_b[layer, i][None, :]
    copies = {}

    def keep(name, outs):
        if emit:
            copies[name] = _Weight(outs[-1])
            return outs[:-1]
        return outs

    hk = w["ret_in"].ncols // 6
    qkg, v = keep("ret_in", _proj(x, w["ret_in"], None, emit=emit, bf16_cols=(2 * hk, 4 * hk)))
    if seq % RET_CHUNK == 0:
        o, s_new = _retention(qkg, v, state0, batch=batch, seq=seq, chunk=RET_CHUNK, n_valid=RET_CHUNK, pos0=pos0)
    else:
        chunk = 2 * SUBLANES_F32
        pad = lambda a: jnp.pad(a.reshape(batch, seq, -1), ((0, 0), (0, chunk - seq), (0, 0))).reshape(batch * chunk, -1)
        o, s_new = _retention(pad(qkg), pad(v), state0, batch=batch, seq=chunk, chunk=chunk,
                              n_valid=seq, pos0=pos0)
        o = o.reshape(batch, chunk, -1)[:, :seq].reshape(m, -1)
    x, xb = keep("ret_out", _proj_norm(o, w["ret_out"], x, g(0, 0), b(0, 0), emit=emit))
    outs = _mlp(x, xb, w["ff1_0"], w["ff2_0"], g(0, 1), b(0, 1), emit=emit)
    if emit:
        copies["ff1_0"], copies["ff2_0"] = _Weight(outs[2]), _Weight(outs[3])
    x, xb = outs[:2]

    q_scale = LOG2_E * (d // SB_HEADS) ** -0.5
    if decode is None:
        k_new, v_new, q = _proj_kvq(xb, w["k"], w["v"], w["sb_q"], q_scale, after)
        o = _sb_prompt(q, k_new, v_new, sb_bias, batch=batch, seq=seq)
    else:
        cache_k, cache_v, page_table = decode
        k_new, = keep("k", _proj(xb, w["k"], [F32], emit=emit))
        v_new, = keep("v", _proj(xb, w["v"], [F32], emit=emit))
        q, = keep("sb_q", _proj(xb, w["sb_q"], [F32], emit=emit, scale=q_scale))
        o = _sb_decode(q, k_new, v_new, cache_k, cache_v, page_table, sb_bias, batch=batch, t_new=seq)
    x, xb = keep("sb_o", _proj_norm(o, w["sb_o"], x, g(1, 0), b(1, 0), emit=emit))
    outs = _mlp(x, xb, w["ff1_1"], w["ff2_1"], g(1, 1), b(1, 1), emit=emit)
    if emit:
        copies["ff1_1"], copies["ff2_1"] = _Weight(outs[2]), _Weight(outs[3])
    return outs[0], s_new, k_new, v_new, copies


def kernel(x_prompt, x_sample, state_ret, cache_k, cache_v, page_table, w_ret_in, w_ret_out, w_kv, w_sb_q,
           w_sb_o, sb_bias, w_ff1, w_ff2, ln_g, ln_b):
    bp, tp, d = x_prompt.shape
    bs, ts, _ = x_sample.shape
    dh = d // SB_HEADS
    n_pages = page_table.shape[1]
    page = cache_k.shape[1]
    weights = {"ret_in": _Weight(w_ret_in, 0), "ret_out": _Weight(w_ret_out, 0),
               "k": _Weight(w_kv, col0=0, ncols=d), "v": _Weight(w_kv, col0=d, ncols=d),
               "sb_q": _Weight(w_sb_q, 0), "sb_o": _Weight(w_sb_o, 0),
               "ff1_0": _Weight(w_ff1, 0), "ff2_0": _Weight(w_ff2, 0),
               "ff1_1": _Weight(w_ff1, 1), "ff2_1": _Weight(w_ff2, 1)}
    norms = (ln_g.astype(F32), ln_b.astype(F32))

    y_s, s_s, k_s, v_s, weights_bf16 = _trunk(
        x_sample.reshape(bs * ts, d), weights, norms, sb_bias[0], batch=bs, seq=ts, pos0=n_pages * page,
        state0=state_ret[0].astype(F32), decode=(cache_k, cache_v, page_table), emit=True)
    y_p, s_p, k_p, v_p, _ = _trunk(
        x_prompt.reshape(bp * tp, d), weights_bf16, norms, sb_bias[0], batch=bp, seq=tp, pos0=0,
        state0=None, decode=None, emit=False, after=y_s[:1, :1])
    heads = lambda a, bb, tt: a.reshape(bb, tt, SB_HEADS, dh)
    return (y_p.reshape(bp, tp, d), y_s.reshape(bs, ts, d),
            s_p[None].astype(state_ret.dtype), heads(k_p, bp, tp), heads(v_p, bp, tp),
            s_s[None].astype(state_ret.dtype), heads(k_s, bs, ts), heads(v_s, bs, ts))
```

```python
import functools

import jax
import jax.numpy as jnp
from jax import lax
from jax.experimental import pallas as pl
from jax.experimental.pallas import tpu as pltpu

F32 = jnp.float32
BF16 = jnp.bfloat16

RET_HEADS = 8
RET_CHUNK = 128
ROPE_BASE = 10000.0
SB_HEADS = 16
LN_EPS = 1e-5
GN_EPS = 1e-6
DEPTH = 2
ALPHA = (2.0 * DEPTH) ** 0.25

V7X_VMEM_BYTES = 64 * 1024 * 1024
VMEM_LIMIT_BYTES = V7X_VMEM_BYTES - 12 * 1024 * 1024
LANES = 128
SUBLANES_F32 = 8
LOG2_E = 1.4426950408889634


def _params(semantics):
    return pltpu.CompilerParams(dimension_semantics=semantics, vmem_limit_bytes=VMEM_LIMIT_BYTES)


def _row_tile(m, want):
    return want if m % want == 0 else m


class _Weight:
    def __init__(self, array, layer=0, col0=0, ncols=None):
        self.array, self.layer, self.col0 = array, layer, col0
        self.rows = array.shape[-2]
        self.ncols = array.shape[-1] - col0 if ncols is None else ncols
        self.is_bf16 = array.dtype == BF16

    def spec(self, block, index_map):
        off = self.col0 // block[1]
        assert self.col0 % block[1] == 0
        if self.array.ndim == 2:
            return pl.BlockSpec(block, lambda *ids: (index_map(*ids)[0], index_map(*ids)[1] + off))
        return pl.BlockSpec((pl.Squeezed(),) + block,
                            lambda *ids: (self.layer, index_map(*ids)[0], index_map(*ids)[1] + off))


def _bf16(x):
    return x if x.dtype == BF16 else x.astype(BF16)


def _proj_kernel(x_ref, w_ref, *refs, n_out, cast_x, emit, scale, bf16_tiles):
    outs = refs[:n_out]
    rest = refs[n_out:]
    if cast_x:
        xb_ref = rest[-1]

        @pl.when(pl.program_id(1) == 0)
        def _():
            xb_ref[...] = x_ref[...].astype(BF16)

    else:
        xb_ref = x_ref

    def product():
        w = _bf16(w_ref[...])
        if emit:
            rest[0][...] = w
        acc = jnp.dot(xb_ref[...], w, preferred_element_type=F32)
        return acc if scale is None else acc * scale

    if bf16_tiles is None:
        acc = product()
        for o in outs:
            o[...] = acc.astype(o.dtype)
        return
    b0, b1 = bf16_tiles
    j = pl.program_id(1)
    in_bf16 = (j >= b0) & (j < b1)

    @pl.when(in_bf16)
    def _():
        outs[1][...] = product().astype(BF16)

    @pl.when(jnp.logical_not(in_bf16))
    def _():
        outs[0][...] = product()


def _proj(x, w, out_dtypes, *, emit=False, scale=None, bf16_cols=None, tm=1024, tn=1024):
    m, k = x.shape
    n = w.ncols
    tm = _row_tile(m, tm)
    tn = _row_tile(n, tn)
    assert not emit or m == tm
    cast_x = x.dtype != BF16
    scratch = [pltpu.VMEM((tm, k), BF16)] if cast_x else []
    emit_specs = [pl.BlockSpec((k, tn), lambda i, j: (0, j))] if emit else []
    emit_shapes = [jax.ShapeDtypeStruct((k, n), BF16)] if emit else []
    if bf16_cols is None:
        bf16_tiles = None
        out_specs = [pl.BlockSpec((tm, tn), lambda i, j: (i, j)) for _ in out_dtypes]
        out_shapes = [jax.ShapeDtypeStruct((m, n), dt) for dt in out_dtypes]
    else:
        c0, c1 = bf16_cols
        assert c0 % tn == 0 and c1 % tn == 0 and 0 < c0 < c1
        b0, b1 = c0 // tn, c1 // tn
        bf16_tiles = (b0, b1)
        f32_tile = lambda i, j: (i, jnp.where(j < b0, j, jnp.where(j >= b1, j - (b1 - b0), b0 - 1)))
        bf16_tile = lambda i, j: (i, jnp.clip(j - b0, 0, b1 - b0 - 1))
        out_specs = [pl.BlockSpec((tm, tn), f32_tile), pl.BlockSpec((tm, tn), bf16_tile)]
        out_shapes = [jax.ShapeDtypeStruct((m, n - (c1 - c0)), F32), jax.ShapeDtypeStruct((m, c1 - c0), BF16)]
        out_dtypes = [F32, BF16]
    return pl.pallas_call(
        functools.partial(_proj_kernel, n_out=len(out_dtypes), cast_x=cast_x, emit=emit, scale=scale,
                          bf16_tiles=bf16_tiles),
        grid=(m // tm, n // tn),
        in_specs=[pl.BlockSpec((tm, k), lambda i, j: (i, 0)),
                  w.spec((k, tn), lambda i, j: (0, j))],
        out_specs=out_specs + emit_specs,
        out_shape=out_shapes + emit_shapes,
        scratch_shapes=scratch,
        compiler_params=_params(("parallel", "arbitrary")),
        name="proj",
    )(x, w.array)


def _proj_kvq_kernel(x_ref, wk_ref, wv_ref, wq_ref, after_ref, kf_ref, kb_ref, vf_ref, vb_ref, q_ref, *, tiles,
                     q_scale):
    del after_ref
    j = pl.program_id(1)
    x = x_ref[...]

    @pl.when(j < tiles)
    def _():
        acc = jnp.dot(x, wk_ref[...], preferred_element_type=F32)
        kf_ref[...] = acc
        kb_ref[...] = acc.astype(BF16)

    @pl.when((j >= tiles) & (j < 2 * tiles))
    def _():
        acc = jnp.dot(x, wv_ref[...], preferred_element_type=F32)
        vf_ref[...] = acc
        vb_ref[...] = acc.astype(BF16)

    @pl.when(j >= 2 * tiles)
    def _():
        q_ref[...] = (jnp.dot(x, wq_ref[...], preferred_element_type=F32) * q_scale).astype(BF16)


def _proj_kvq(xb, wk, wv, wq, q_scale, after, *, tm=1024, tn=512):
    m, k = xb.shape
    n = wk.ncols
    assert wk.is_bf16 and wv.is_bf16 and wq.is_bf16 and wv.ncols == n and wq.ncols == n
    tiles = n // tn

    def tile_of(which):
        return lambda i, j: (i, jnp.clip(j - which * tiles, 0, tiles - 1))

    w_specs = [w.spec((k, tn), lambda i, j, t=tile_of(which): (0, t(i, j)[1]))
               for which, w in enumerate((wk, wv, wq))]
    out_spec = lambda which: pl.BlockSpec((tm, tn), tile_of(which))
    shape = lambda dt: jax.ShapeDtypeStruct((m, n), dt)
    return pl.pallas_call(
        functools.partial(_proj_kvq_kernel, tiles=tiles, q_scale=q_scale),
        grid=(m // tm, 3 * tiles),
        in_specs=[pl.BlockSpec((tm, k), lambda i, j: (i, 0))] + w_specs + [pl.BlockSpec((1, 1), lambda i, j: (0, 0))],
        out_specs=[out_spec(0), out_spec(0), out_spec(1), out_spec(1), out_spec(2)],
        out_shape=[shape(F32), shape(BF16), shape(F32), shape(BF16), shape(BF16)],
        compiler_params=_params(("parallel", "arbitrary")),
        name="proj_kvq",
    )(xb, wk.array, wv.array, wq.array, after)


def _post_norm_store(res, sub, g_ref, b_ref, of_ref, ob_ref):
    y = ALPHA * res + sub
    mu = jnp.mean(y, axis=-1, keepdims=True)
    yc = y - mu
    var = jnp.mean(yc * yc, axis=-1, keepdims=True)
    out = yc * lax.rsqrt(var + LN_EPS) * g_ref[...] + b_ref[...]
    of_ref[...] = out
    ob_ref[...] = out.astype(BF16)


ACC_COLS = 512


def _zero_on_first(acc_ref, step):
    @pl.when(step == 0)
    def _():
        acc_ref[...] = jnp.zeros_like(acc_ref)


def _accumulate_dot(acc_ref, lhs, w_ref, wq_ref=None):
    n = acc_ref.shape[1]
    tn = ACC_COLS if n % ACC_COLS == 0 else n
    for j in range(n // tn):
        cols = slice(j * tn, (j + 1) * tn)
        w = _bf16(w_ref[:, cols])
        if wq_ref is not None:
            wq_ref[:, cols] = w
        acc_ref[:, cols] += jnp.dot(lhs, w, preferred_element_type=F32)


def _proj_norm_kernel(x_ref, w_ref, res_ref, g_ref, b_ref, of_ref, ob_ref, *rest, nk, emit):
    wq_ref = rest[0] if emit else None
    if nk == 1:
        w = _bf16(w_ref[...])
        if emit:
            wq_ref[...] = w
        sub = jnp.dot(x_ref[...], w, preferred_element_type=F32)
        _post_norm_store(res_ref[...], sub, g_ref, b_ref, of_ref, ob_ref)
        return
    acc_ref = rest[-1]
    k = pl.program_id(1)
    _zero_on_first(acc_ref, k)
    _accumulate_dot(acc_ref, x_ref[...], w_ref, wq_ref)

    @pl.when(k == nk - 1)
    def _():
        _post_norm_store(res_ref[...], acc_ref[...], g_ref, b_ref, of_ref, ob_ref)


def _proj_norm(x, w, res, g, b, *, emit=False, tm=512):
    m, kdim = x.shape
    n = w.ncols
    tm = _row_tile(m, tm)
    assert not emit or m == tm
    tk = kdim if w.is_bf16 else min(kdim, 1024)
    nk = kdim // tk
    row = lambda i, k: (i, 0)
    emit_specs = [pl.BlockSpec((tk, n), lambda i, k: (k, 0))] if emit else []
    emit_shapes = [jax.ShapeDtypeStruct((kdim, n), BF16)] if emit else []
    w_spec = w.spec((tk, n), lambda i, k: (k, 0))
    if nk == 1:
        w_spec = pl.BlockSpec(w_spec.block_shape, w_spec.index_map, pipeline_mode=pl.Buffered(1))
    return pl.pallas_call(
        functools.partial(_proj_norm_kernel, nk=nk, emit=emit),
        grid=(m // tm, nk),
        in_specs=[pl.BlockSpec((tm, tk), lambda i, k: (i, k)),
                  w_spec,
                  pl.BlockSpec((tm, n), row),
                  pl.BlockSpec((1, n), lambda i, k: (0, 0)),
                  pl.BlockSpec((1, n), lambda i, k: (0, 0))],
        out_specs=[pl.BlockSpec((tm, n), row), pl.BlockSpec((tm, n), row)] + emit_specs,
        out_shape=[jax.ShapeDtypeStruct((m, n), F32), jax.ShapeDtypeStruct((m, n), BF16)] + emit_shapes,
        scratch_shapes=[pltpu.VMEM((tm, n), F32)] if nk > 1 else [],
        compiler_params=_params(("parallel", "arbitrary")),
        name="proj_norm",
    )(x, w.array, res, g, b)


def _mlp_kernel(x_ref, xb_ref, w1_ref, w2_ref, g_ref, b_ref, of_ref, ob_ref, *rest, nf, emit):
    acc_ref = rest[-1]
    f = pl.program_id(1)
    _zero_on_first(acc_ref, f)
    w1 = _bf16(w1_ref[...])
    if emit:
        rest[0][...] = w1
    h = jnp.dot(xb_ref[...], w1, preferred_element_type=F32)
    h = jnp.square(jnp.maximum(h, 0.0)).astype(BF16)
    _accumulate_dot(acc_ref, h, w2_ref, rest[1] if emit else None)

    @pl.when(f == nf - 1)
    def _():
        _post_norm_store(x_ref[...], acc_ref[...], g_ref, b_ref, of_ref, ob_ref)


def _mlp(x, xb, w1, w2, g, b, *, emit=False, tm=512):
    m, d = x.shape
    dff = w1.ncols
    tm = _row_tile(m, tm)
    assert not emit or m == tm
    tf = 1024 if w1.is_bf16 else 512
    nf = dff // tf
    row = lambda i, f: (i, 0)
    emit_specs = [pl.BlockSpec((d, tf), lambda i, f: (0, f)), pl.BlockSpec((tf, d), lambda i, f: (f, 0))] if emit else []
    emit_shapes = [jax.ShapeDtypeStruct((d, dff), BF16), jax.ShapeDtypeStruct((dff, d), BF16)] if emit else []
    return pl.pallas_call(
        functools.partial(_mlp_kernel, nf=nf, emit=emit),
        grid=(m // tm, nf),
        in_specs=[pl.BlockSpec((tm, d), row),
                  pl.BlockSpec((tm, d), row),
                  w1.spec((d, tf), lambda i, f: (0, f)),
                  w2.spec((tf, d), lambda i, f: (f, 0)),
                  pl.BlockSpec((1, d), lambda i, f: (0, 0)),
                  pl.BlockSpec((1, d), lambda i, f: (0, 0))],
        out_specs=[pl.BlockSpec((tm, d), row), pl.BlockSpec((tm, d), row)] + emit_specs,
        out_shape=[jax.ShapeDtypeStruct((m, d), F32), jax.ShapeDtypeStruct((m, d), BF16)] + emit_shapes,
        scratch_shapes=[pltpu.VMEM((tm, d), F32)],
        compiler_params=_params(("parallel", "arbitrary")),
        name="mlp",
    )(x, xb, w1.array, w2.array, g, b)


RET_HEADS_PER_STEP = 8


def _rotary(x, cos, sin):
    half = x.shape[-1] // 2
    x1, x2 = x[:, :half], x[:, half:]
    return jnp.concatenate([x1 * cos - x2 * sin, x1 * sin + x2 * cos], axis=-1)


def _retention_kernel(*refs, n_chunks, has_state0, q_scale, hb, dk_dim, dv_dim):
    if has_state0:
        s0_ref, refs = refs[0], refs[1:]
    (q_ref, k_ref, v_ref, g_ref, cos_ref, sin_ref, din_ref, dq_ref, dk_ref, dc_ref,
     o_ref, sout_ref, s_ref) = refs
    c = pl.program_id(2)

    @pl.when(c == 0)
    def _():
        if has_state0:
            s_ref[...] = s0_ref[0]
        else:
            s_ref[...] = jnp.zeros_like(s_ref)

    cos, sin = cos_ref[...], sin_ref[...]
    for i in range(hb):
        kcols = slice(i * dk_dim, (i + 1) * dk_dim)
        vcols = slice(i * dv_dim, (i + 1) * dv_dim)
        q = (_rotary(q_ref[:, kcols], cos, sin) * q_scale).astype(BF16)
        k = _rotary(k_ref[:, kcols], cos, sin)
        v = v_ref[:, vcols]
        state = s_ref[i]

        scores = lax.dot_general(q, k.astype(BF16), (((1,), (1,)), ((), ())),
                                 preferred_element_type=F32) * din_ref[i]
        inner = jnp.dot(scores.astype(BF16), v, preferred_element_type=F32)
        cross = jnp.dot(q, state.astype(BF16), preferred_element_type=F32) * dq_ref[i]
        kd = (k * dk_ref[i]).astype(BF16)
        s_ref[i] = dc_ref[i] * state + lax.dot_general(kd, v, (((0,), (0,)), ((), ())),
                                                       preferred_element_type=F32)

        o = inner + cross
        mu = jnp.mean(o, axis=-1, keepdims=True)
        oc = o - mu
        var = jnp.mean(oc * oc, axis=-1, keepdims=True)
        gate = g_ref[:, vcols]
        gate = gate / (1.0 + jnp.exp(-gate))
        o_ref[:, vcols] = (gate * (oc * lax.rsqrt(var + GN_EPS))).astype(o_ref.dtype)

    @pl.when(c == n_chunks - 1)
    def _():
        sout_ref[0] = s_ref[...]


def _retention_tables(chunk, n_valid):
    h = RET_HEADS
    log_g = jnp.log1p(-jnp.power(2.0, -5.0 - jnp.arange(h, dtype=F32)))
    idx = jnp.arange(chunk, dtype=F32)
    diff = idx[:, None] - idx[None, :]
    din = jnp.where(diff[None] >= 0, jnp.exp(log_g[:, None, None] * jnp.maximum(diff, 0.0)[None]), 0.0)
    dq = jnp.exp(log_g[:, None] * (idx + 1.0)[None, :])[..., None]
    dk = jnp.exp(log_g[:, None] * (n_valid - 1.0 - idx)[None, :])[..., None]
    dc = jnp.exp(log_g * n_valid).reshape(h, 1, 1)
    return din, dq, dk, dc


def _rotary_tables(pos, half):
    inv = ROPE_BASE ** (-jnp.arange(half, dtype=F32) / half)
    ang = pos.astype(F32)[:, None] * inv[None, :]
    return jnp.cos(ang), jnp.sin(ang)


def _retention(qkg, v, state0, *, batch, seq, chunk, n_valid, pos0):
    h = RET_HEADS
    hb = RET_HEADS_PER_STEP
    ng = h // hb
    dv_dim = v.shape[1] // h
    dk_dim = dv_dim // 2
    assert qkg.shape[1] == 2 * h * dk_dim + h * dv_dim
    n_chunks = seq // chunk
    cos, sin = _rotary_tables(pos0 + jnp.arange(seq, dtype=jnp.int32), dk_dim // 2)
    din, dq, dk, dc = _retention_tables(chunk, n_valid)
    has_state0 = state0 is not None

    row = lambda b, hg, c: b * n_chunks + c
    per_group = lambda b, hg, c: (hg, 0, 0)
    in_specs = [
        pl.BlockSpec((chunk, hb * dk_dim), lambda b, hg, c: (row(b, hg, c), hg)),
        pl.BlockSpec((chunk, hb * dk_dim), lambda b, hg, c: (row(b, hg, c), ng + hg)),
        pl.BlockSpec((chunk, hb * dv_dim), lambda b, hg, c: (row(b, hg, c), hg)),
        pl.BlockSpec((chunk, hb * dv_dim), lambda b, hg, c: (row(b, hg, c), ng + hg)),
        pl.BlockSpec((chunk, dk_dim // 2), lambda b, hg, c: (c, 0)),
        pl.BlockSpec((chunk, dk_dim // 2), lambda b, hg, c: (c, 0)),
        pl.BlockSpec((hb, chunk, chunk), per_group),
        pl.BlockSpec((hb, chunk, 1), per_group),
        pl.BlockSpec((hb, chunk, 1), per_group),
        pl.BlockSpec((hb, 1, 1), per_group),
    ]
    args = [qkg, qkg, v, qkg, cos, sin, din, dq, dk, dc]
    state_spec = pl.BlockSpec((1, hb, dk_dim, dv_dim), lambda b, hg, c: (b, hg, 0, 0))
    if has_state0:
        in_specs.insert(0, state_spec)
        args.insert(0, state0)
    o, s_out = pl.pallas_call(
        functools.partial(_retention_kernel, n_chunks=n_chunks, has_state0=has_state0,
                          q_scale=dk_dim ** -0.5, hb=hb, dk_dim=dk_dim, dv_dim=dv_dim),
        grid=(batch, ng, n_chunks),
        in_specs=in_specs,
        out_specs=[pl.BlockSpec((chunk, hb * dv_dim), lambda b, hg, c: (row(b, hg, c), hg)), state_spec],
        out_shape=[jax.ShapeDtypeStruct((batch * seq, h * dv_dim), BF16),
                   jax.ShapeDtypeStruct((batch, h, dk_dim, dv_dim), F32)],
        scratch_shapes=[pltpu.VMEM((hb, dk_dim, dv_dim), F32)],
        compiler_params=_params(("parallel", "parallel", "arbitrary")),
        name="retention",
    )(*args)
    return o, s_out


def _softplus2(z2):
    return jnp.maximum(z2, 0.0) + jnp.log2(1.0 + jnp.exp2(-jnp.abs(z2)))


_NT = (((1,), (1,)), ((), ()))


def _sb_prompt_kernel(q_ref, k_ref, v_ref, bias_ref, o_ref, *, tq, tk):
    qi = pl.program_id(2)
    dh = q_ref.shape[-1]
    nsub = tq // tk
    q = q_ref[...]
    bias2 = bias_ref[0] * LOG2_E

    r = lax.broadcasted_iota(jnp.int32, (tk, tk), 0)
    c = lax.broadcasted_iota(jnp.int32, (tk, tk), 1)
    suffix = jnp.where(r > c, 1.0, 0.0).astype(BF16)
    causal = c < r

    def run(groups):
        raws = [[lax.dot_general(qr, k_ref[pl.ds(st, n * tk), :], _NT, preferred_element_type=F32)
                 for st, n in segs] for qr, segs, _, _, _ in groups]
        tiles = []
        for raw_segs, (_, segs, diagonal, _, _) in zip(raws, groups):
            group_tiles = []
            for raw, (_, n) in zip(raw_segs, segs):
                for s in reversed(range(n)):
                    z2 = raw[:, s * tk:(s + 1) * tk] + bias2
                    sp2 = _softplus2(z2)
                    masked = diagonal and not group_tiles
                    if masked:
                        sp2 = jnp.where(causal, sp2, 0.0)
                    group_tiles.append((z2 - sp2, sp2.astype(BF16), jnp.sum(sp2, axis=-1, keepdims=True), masked))
            tiles.append(group_tiles)
        sums = [[jnp.dot(t[1], suffix, preferred_element_type=F32) for t in group_tiles] for group_tiles in tiles]
        out = []
        for group_tiles, group_sums, (_, segs, _, carry, acc) in zip(tiles, sums, groups):
            ws = []
            for (logit, _, total, masked), in_tile in zip(group_tiles, group_sums):
                w = jnp.exp2(logit - (in_tile + carry))
                if masked:
                    w = jnp.where(causal, w, 0.0)
                ws.append(w.astype(BF16))
                carry = carry + total
            out.append((carry, ws, acc, segs))
        results = []
        for carry, ws, acc, segs in out:
            done = 0
            for st, n in segs:
                w_seg = jnp.concatenate(ws[done:done + n][::-1], axis=-1)
                v_seg = v_ref[pl.ds(st, n * tk), :]
                done += n
                acc = acc + jnp.dot(w_seg, v_seg, preferred_element_type=F32)
            results.append((carry, acc))
        return results

    q_groups = [q[g * tk:(g + 1) * tk] for g in range(nsub)]
    chunk_start = lambda j: pl.multiple_of(j * tq, tq)
    state = tuple(run([(q_groups[g], [(chunk_start(qi), g + 1)], True,
                        jnp.zeros((tk, 1), F32), jnp.zeros((tk, dh), F32)) for g in range(nsub)]))

    def chunks(js, st):
        segs = [(chunk_start(j), nsub) for j in js]
        return tuple(run([(q_groups[g], segs, False, st[g][0], st[g][1]) for g in range(nsub)]))

    odd = qi % 2
    state = lax.fori_loop(0, odd, lambda it, st: chunks([qi - 1], st), state)
    first = qi - 1 - odd
    state = lax.fori_loop(0, qi // 2, lambda it, st: chunks([first - 2 * it, first - 2 * it - 1], st), state)
    for g in range(nsub):
        o_ref[g * tk:(g + 1) * tk, :] = state[g][1].astype(o_ref.dtype)


def _sb_prompt(q, k, v, bias, *, batch, seq, tq=512, tk=256):
    h = SB_HEADS
    dh = q.shape[1] // h
    nq = seq // tq
    return pl.pallas_call(
        functools.partial(_sb_prompt_kernel, tq=tq, tk=tk),
        grid=(batch, h, nq),
        in_specs=[pl.BlockSpec((tq, dh), lambda b, hh, i: (b * nq + i, hh)),
                  pl.BlockSpec((seq, dh), lambda b, hh, i: (b, hh)),
                  pl.BlockSpec((seq, dh), lambda b, hh, i: (b, hh)),
                  pl.BlockSpec((1, 1, 1), lambda b, hh, i: (hh, 0, 0))],
        out_specs=pl.BlockSpec((tq, dh), lambda b, hh, i: (b * nq + i, hh)),
        out_shape=jax.ShapeDtypeStruct(q.shape, BF16),
        compiler_params=_params(("parallel", "parallel", "parallel")),
        name="sb_prompt",
    )(q, k, v, bias.reshape(h, 1, 1))


def _sb_decode_kernel(pt_ref, q_ref, bias_ref, kn_ref, vn_ref, *refs, pages_per_step, n_steps, page, heads,
                      t_new):
    del pt_ref
    k_refs = refs[:pages_per_step]
    v_refs = refs[pages_per_step:2 * pages_per_step]
    o_ref, acc_ref, carry_ref = refs[2 * pages_per_step:]
    p = pl.program_id(1)
    hh = SUBLANES_F32
    n_groups = heads // hh
    rows = hh * t_new
    lanes = page * hh
    n_tiles = lanes // LANES
    dh = q_ref.shape[-1]

    rr = lax.broadcasted_iota(jnp.int32, (LANES, 2 * LANES), 0)
    cc = lax.broadcasted_iota(jnp.int32, (LANES, 2 * LANES), 1)
    suffix = jnp.where((cc >= LANES) | (rr // hh > cc // hh), 1.0, 0.0).astype(BF16)
    lane = lax.broadcasted_iota(jnp.int32, (rows, lanes), 1)
    row = lax.broadcasted_iota(jnp.int32, (rows, lanes), 0)
    own = lane % hh == row // t_new

    def rows_of(ref, grp):
        return ref[0, :, pl.ds(grp * hh, hh), :].reshape(lanes, dh).astype(BF16)

    def update(page_refs, masked):
        keep = own
        if masked:
            keep = own & (lane // hh < row % t_new)
        work = [(k_ref, v_ref, grp) for grp in range(n_groups) for k_ref, v_ref in page_refs]
        z2s = [lax.dot_general(q_ref[0, grp], rows_of(k_ref, grp), _NT, preferred_element_type=F32)
               + bias_ref[grp] * LOG2_E for k_ref, _, grp in work]
        sp2s = [jnp.where(keep, _softplus2(z2), 0.0) for z2 in z2s]
        sums = []
        for sp2 in sp2s:
            stacked = jnp.concatenate([sp2[:, j * LANES:(j + 1) * LANES] for j in range(n_tiles)], axis=0)
            sums.append(jnp.dot(stacked.astype(BF16), suffix, preferred_element_type=F32))
        ws = []
        run = None
        for i, (_, _, grp) in enumerate(work):
            if i % len(page_refs) == 0:
                run = carry_ref[grp]
            later = [None] * n_tiles
            for j in reversed(range(n_tiles)):
                tile_sums = sums[i][j * rows:(j + 1) * rows]
                later[j] = tile_sums[:, :LANES] + run
                run = run + tile_sums[:, LANES:]
            if (i + 1) % len(page_refs) == 0:
                carry_ref[grp] = run
            w = jnp.exp2(z2s[i] - sp2s[i] - jnp.concatenate(later, axis=-1))
            ws.append(jnp.where(keep, w, 0.0).astype(BF16))
        for grp in range(n_groups):
            acc = acc_ref[grp]
            for i, (_, v_ref, g2) in enumerate(work):
                if g2 == grp:
                    acc = acc + jnp.dot(ws[i], rows_of(v_ref, grp), preferred_element_type=F32)
            acc_ref[grp] = acc

    @pl.when(p == 0)
    def _():
        acc_ref[...] = jnp.zeros_like(acc_ref)
        carry_ref[...] = jnp.zeros_like(carry_ref)
        update([(kn_ref, vn_ref)], True)

    update(list(zip(k_refs, v_refs)), False)

    @pl.when(p == n_steps - 1)
    def _():
        o_ref[0] = acc_ref[...]


def _sb_decode(q, k_new, v_new, cache_k, cache_v, page_table, bias, *, batch, t_new, pages_per_step=8):
    h = SB_HEADS
    hh = SUBLANES_F32
    n_pool, page, _, dh = cache_k.shape
    assert h % hh == 0 and (page * hh) % LANES == 0 and LANES % hh == 0
    n_groups = h // hh
    rows = hh * t_new
    n_pages = page_table.shape[1]
    n_steps = n_pages // pages_per_step
    q_rows = jnp.transpose(q.reshape(batch, t_new, n_groups, hh, dh), (0, 2, 3, 1, 4))
    q_rows = q_rows.reshape(batch, n_groups, rows, dh).astype(BF16)
    bias_rows = jnp.repeat(bias.astype(F32).reshape(n_groups, hh), t_new, axis=1)[..., None]
    pad = lambda a: jnp.pad(a.reshape(batch, t_new, h, dh), ((0, 0), (0, page - t_new), (0, 0), (0, 0)))
    kn, vn = pad(k_new), pad(v_new)

    def page_map(g):
        return lambda b, p, pt: (pt[b, n_pages - 1 - (p * pages_per_step + g)], 0, 0, 0)

    page_specs = [pl.BlockSpec((1, page, h, dh), page_map(g)) for g in range(pages_per_step)]
    per_batch = lambda b, p, pt: (b, 0, 0, 0)
    acc = pl.pallas_call(
        functools.partial(_sb_decode_kernel, pages_per_step=pages_per_step, n_steps=n_steps, page=page, heads=h,
                          t_new=t_new),
        grid_spec=pltpu.PrefetchScalarGridSpec(
            num_scalar_prefetch=1,
            grid=(batch, n_steps),
            in_specs=[pl.BlockSpec((1, n_groups, rows, dh), per_batch),
                      pl.BlockSpec((n_groups, rows, 1), lambda b, p, pt: (0, 0, 0)),
                      pl.BlockSpec((1, page, h, dh), per_batch),
                      pl.BlockSpec((1, page, h, dh), per_batch)] + page_specs + page_specs,
            out_specs=pl.BlockSpec((1, n_groups, rows, dh), per_batch),
            scratch_shapes=[pltpu.VMEM((n_groups, rows, dh), F32), pltpu.VMEM((n_groups, rows, LANES), F32)],
        ),
        out_shape=jax.ShapeDtypeStruct((batch, n_groups, rows, dh), F32),
        compiler_params=_params(("parallel", "arbitrary")),
        name="sb_decode",
    )(page_table, q_rows, bias_rows, kn, vn, *([cache_k] * pages_per_step), *([cache_v] * pages_per_step))
    o = jnp.transpose(acc.reshape(batch, n_groups, hh, t_new, dh), (0, 3, 1, 2, 4))
    return o.reshape(batch * t_new, h * dh).astype(BF16)


def _trunk(x, w, norms, sb_bias, *, batch, seq, pos0, state0, decode, emit, after=None):
    ln_g, ln_b = norms
    m, d = x.shape
    g = lambda layer, i: ln_g[layer, i][None, :]
    b = lambda layer, i: ln_b[layer, i][None, :]
    copies = {}

    def keep(name, outs):
        if emit:
            copies[name] = _Weight(outs[-1])
            return outs[:-1]
        return outs

    hk = w["ret_in"].ncols // 6
    qkg, v = keep("ret_in", _proj(x, w["ret_in"], None, emit=emit, bf16_cols=(2 * hk, 4 * hk)))
    if seq % RET_CHUNK == 0:
        o, s_new = _retention(qkg, v, state0, batch=batch, seq=seq, chunk=RET_CHUNK, n_valid=RET_CHUNK, pos0=pos0)
    else:
        chunk = 2 * SUBLANES_F32
        pad = lambda a: jnp.pad(a.reshape(batch, seq, -1), ((0, 0), (0, chunk - seq), (0, 0))).reshape(batch * chunk, -1)
        o, s_new = _retention(pad(qkg), pad(v), state0, batch=batch, seq=chunk, chunk=chunk,
                              n_valid=seq, pos0=pos0)
        o = o.reshape(batch, chunk, -1)[:, :seq].reshape(m, -1)
    x, xb = keep("ret_out", _proj_norm(o, w["ret_out"], x, g(0, 0), b(0, 0), emit=emit))
    outs = _mlp(x, xb, w["ff1_0"], w["ff2_0"], g(0, 1), b(0, 1), emit=emit)
    if emit:
        copies["ff1_0"], copies["ff2_0"] = _Weight(outs[2]), _Weight(outs[3])
    x, xb = outs[:2]

    q_scale = LOG2_E * (d // SB_HEADS) ** -0.5
    if decode is None:
        k_new, kb, v_new, vb, q = _proj_kvq(xb, w["k"], w["v"], w["sb_q"], q_scale, after)
        o = _sb_prompt(q, kb, vb, sb_bias, batch=batch, seq=seq)
    else:
        cache_k, cache_v, page_table = decode
        k_new, = keep("k", _proj(xb, w["k"], [F32], emit=emit))
        v_new, = keep("v", _proj(xb, w["v"], [F32], emit=emit))
        q, = keep("sb_q", _proj(xb, w["sb_q"], [F32], emit=emit, scale=q_scale))
        o = _sb_decode(q, k_new, v_new, cache_k, cache_v, page_table, sb_bias, batch=batch, t_new=seq)
    x, xb = keep("sb_o", _proj_norm(o, w["sb_o"], x, g(1, 0), b(1, 0), emit=emit))
    outs = _mlp(x, xb, w["ff1_1"], w["ff2_1"], g(1, 1), b(1, 1), emit=emit)
    if emit:
        copies["ff1_1"], copies["ff2_1"] = _Weight(outs[2]), _Weight(outs[3])
    return outs[0], s_new, k_new, v_new, copies


def kernel(x_prompt, x_sample, state_ret, cache_k, cache_v, page_table, w_ret_in, w_ret_out, w_kv, w_sb_q,
           w_sb_o, sb_bias, w_ff1, w_ff2, ln_g, ln_b):
    bp, tp, d = x_prompt.shape
    bs, ts, _ = x_sample.shape
    dh = d // SB_HEADS
    n_pages = page_table.shape[1]
    page = cache_k.shape[1]
    weights = {"ret_in": _Weight(w_ret_in, 0), "ret_out": _Weight(w_ret_out, 0),
               "k": _Weight(w_kv, col0=0, ncols=d), "v": _Weight(w_kv, col0=d, ncols=d),
               "sb_q": _Weight(w_sb_q, 0), "sb_o": _Weight(w_sb_o, 0),
               "ff1_0": _Weight(w_ff1, 0), "ff2_0": _Weight(w_ff2, 0),
               "ff1_1": _Weight(w_ff1, 1), "ff2_1": _Weight(w_ff2, 1)}
    norms = (ln_g.astype(F32), ln_b.astype(F32))

    y_s, s_s, k_s, v_s, weights_bf16 = _trunk(
        x_sample.reshape(bs * ts, d), weights, norms, sb_bias[0], batch=bs, seq=ts, pos0=n_pages * page,
        state0=state_ret[0].astype(F32), decode=(cache_k, cache_v, page_table), emit=True)
    y_p, s_p, k_p, v_p, _ = _trunk(
        x_prompt.reshape(bp * tp, d), weights_bf16, norms, sb_bias[0], batch=bp, seq=tp, pos0=0,
        state0=None, decode=None, emit=False, after=y_s[:1, :1])
    heads = lambda a, bb, tt: a.reshape(bb, tt, SB_HEADS, dh)
    return (y_p.reshape(bp, tp, d), y_s.reshape(bs, ts, d),
            s_p[None].astype(state_ret.dtype), heads(k_p, bp, tp), heads(v_p, bp, tp),
            s_s[None].astype(state_ret.dtype), heads(k_s, bs, ts), heads(v_s, bs, ts))
```

```python
import functools

import jax
import jax.numpy as jnp
from jax import lax
from jax.experimental import pallas as pl
from jax.experimental.pallas import tpu as pltpu

F32 = jnp.float32
BF16 = jnp.bfloat16

RET_HEADS = 8
RET_CHUNK = 128
ROPE_BASE = 10000.0
SB_HEADS = 16
LN_EPS = 1e-5
GN_EPS = 1e-6
DEPTH = 2
ALPHA = (2.0 * DEPTH) ** 0.25

V7X_VMEM_BYTES = 64 * 1024 * 1024
VMEM_LIMIT_BYTES = V7X_VMEM_BYTES - 12 * 1024 * 1024
LANES = 128
SUBLANES_F32 = 8
LOG2_E = 1.4426950408889634


def _params(semantics):
    return pltpu.CompilerParams(dimension_semantics=semantics, vmem_limit_bytes=VMEM_LIMIT_BYTES)


def _row_tile(m, want):
    return want if m % want == 0 else m


class _Weight:
    def __init__(self, array, layer=0, col0=0, ncols=None):
        self.array, self.layer, self.col0 = array, layer, col0
        self.rows = array.shape[-2]
        self.ncols = array.shape[-1] - col0 if ncols is None else ncols
        self.is_bf16 = array.dtype == BF16

    def spec(self, block, index_map):
        off = self.col0 // block[1]
        assert self.col0 % block[1] == 0
        if self.array.ndim == 2:
            return pl.BlockSpec(block, lambda *ids: (index_map(*ids)[0], index_map(*ids)[1] + off))
        return pl.BlockSpec((pl.Squeezed(),) + block,
                            lambda *ids: (self.layer, index_map(*ids)[0], index_map(*ids)[1] + off))


def _bf16(x):
    return x if x.dtype == BF16 else x.astype(BF16)


def _proj_kernel(x_ref, w_ref, *refs, n_out, cast_x, emit, scale, bf16_tiles):
    outs = refs[:n_out]
    rest = refs[n_out:]
    if cast_x:
        xb_ref = rest[-1]

        @pl.when(pl.program_id(1) == 0)
        def _():
            xb_ref[...] = x_ref[...].astype(BF16)

    else:
        xb_ref = x_ref

    def product():
        w = _bf16(w_ref[...])
        if emit:
            rest[0][...] = w
        acc = jnp.dot(xb_ref[...], w, preferred_element_type=F32)
        return acc if scale is None else acc * scale

    if bf16_tiles is None:
        acc = product()
        for o in outs:
            o[...] = acc.astype(o.dtype)
        return
    b0, b1 = bf16_tiles
    j = pl.program_id(1)
    in_bf16 = (j >= b0) & (j < b1)

    @pl.when(in_bf16)
    def _():
        outs[1][...] = product().astype(BF16)

    @pl.when(jnp.logical_not(in_bf16))
    def _():
        outs[0][...] = product()


def _proj(x, w, out_dtypes, *, emit=False, scale=None, bf16_cols=None, tm=1024, tn=1024):
    m, k = x.shape
    n = w.ncols
    tm = _row_tile(m, tm)
    tn = _row_tile(n, tn)
    assert not emit or m == tm
    cast_x = x.dtype != BF16
    scratch = [pltpu.VMEM((tm, k), BF16)] if cast_x else []
    emit_specs = [pl.BlockSpec((k, tn), lambda i, j: (0, j))] if emit else []
    emit_shapes = [jax.ShapeDtypeStruct((k, n), BF16)] if emit else []
    if bf16_cols is None:
        bf16_tiles = None
        out_specs = [pl.BlockSpec((tm, tn), lambda i, j: (i, j)) for _ in out_dtypes]
        out_shapes = [jax.ShapeDtypeStruct((m, n), dt) for dt in out_dtypes]
    else:
        c0, c1 = bf16_cols
        assert c0 % tn == 0 and c1 % tn == 0 and 0 < c0 < c1
        b0, b1 = c0 // tn, c1 // tn
        bf16_tiles = (b0, b1)
        f32_tile = lambda i, j: (i, jnp.where(j < b0, j, jnp.where(j >= b1, j - (b1 - b0), b0 - 1)))
        bf16_tile = lambda i, j: (i, jnp.clip(j - b0, 0, b1 - b0 - 1))
        out_specs = [pl.BlockSpec((tm, tn), f32_tile), pl.BlockSpec((tm, tn), bf16_tile)]
        out_shapes = [jax.ShapeDtypeStruct((m, n - (c1 - c0)), F32), jax.ShapeDtypeStruct((m, c1 - c0), BF16)]
        out_dtypes = [F32, BF16]
    return pl.pallas_call(
        functools.partial(_proj_kernel, n_out=len(out_dtypes), cast_x=cast_x, emit=emit, scale=scale,
                          bf16_tiles=bf16_tiles),
        grid=(m // tm, n // tn),
        in_specs=[pl.BlockSpec((tm, k), lambda i, j: (i, 0)),
                  w.spec((k, tn), lambda i, j: (0, j))],
        out_specs=out_specs + emit_specs,
        out_shape=out_shapes + emit_shapes,
        scratch_shapes=scratch,
        compiler_params=_params(("parallel", "arbitrary")),
        name="proj",
    )(x, w.array)


def _proj_kvq_kernel(x_ref, wk_ref, wv_ref, wq_ref, after_ref, kf_ref, kb_ref, vf_ref, vb_ref, q_ref, *, tiles,
                     q_scale):
    del after_ref
    j = pl.program_id(1)
    x = x_ref[...]

    @pl.when(j < tiles)
    def _():
        acc = jnp.dot(x, wk_ref[...], preferred_element_type=F32)
        kf_ref[...] = acc
        kb_ref[...] = acc.astype(BF16)

    @pl.when((j >= tiles) & (j < 2 * tiles))
    def _():
        acc = jnp.dot(x, wv_ref[...], preferred_element_type=F32)
        vf_ref[...] = acc
        vb_ref[...] = acc.astype(BF16)

    @pl.when(j >= 2 * tiles)
    def _():
        q_ref[...] = (jnp.dot(x, wq_ref[...], preferred_element_type=F32) * q_scale).astype(BF16)


def _proj_kvq(xb, wk, wv, wq, q_scale, after, *, tm=1024, tn=512):
    m, k = xb.shape
    n = wk.ncols
    assert wk.is_bf16 and wv.is_bf16 and wq.is_bf16 and wv.ncols == n and wq.ncols == n
    tiles = n // tn

    def tile_of(which):
        return lambda i, j: (i, jnp.clip(j - which * tiles, 0, tiles - 1))

    w_specs = [w.spec((k, tn), lambda i, j, t=tile_of(which): (0, t(i, j)[1]))
               for which, w in enumerate((wk, wv, wq))]
    out_spec = lambda which: pl.BlockSpec((tm, tn), tile_of(which))
    shape = lambda dt: jax.ShapeDtypeStruct((m, n), dt)
    return pl.pallas_call(
        functools.partial(_proj_kvq_kernel, tiles=tiles, q_scale=q_scale),
        grid=(m // tm, 3 * tiles),
        in_specs=[pl.BlockSpec((tm, k), lambda i, j: (i, 0))] + w_specs + [pl.BlockSpec((1, 1), lambda i, j: (0, 0))],
        out_specs=[out_spec(0), out_spec(0), out_spec(1), out_spec(1), out_spec(2)],
        out_shape=[shape(F32), shape(BF16), shape(F32), shape(BF16), shape(BF16)],
        compiler_params=_params(("parallel", "arbitrary")),
        name="proj_kvq",
    )(xb, wk.array, wv.array, wq.array, after)


def _post_norm_store(res, sub, g_ref, b_ref, of_ref, ob_ref):
    y = ALPHA * res + sub
    mu = jnp.mean(y, axis=-1, keepdims=True)
    yc = y - mu
    var = jnp.mean(yc * yc, axis=-1, keepdims=True)
    out = yc * lax.rsqrt(var + LN_EPS) * g_ref[...] + b_ref[...]
    of_ref[...] = out
    if ob_ref is not None:
        ob_ref[...] = out.astype(BF16)


ACC_COLS = 512


def _zero_on_first(acc_ref, step):
    @pl.when(step == 0)
    def _():
        acc_ref[...] = jnp.zeros_like(acc_ref)


def _accumulate_dot(acc_ref, lhs, w_ref, wq_ref=None):
    n = acc_ref.shape[1]
    tn = ACC_COLS if n % ACC_COLS == 0 else n
    for j in range(n // tn):
        cols = slice(j * tn, (j + 1) * tn)
        w = _bf16(w_ref[:, cols])
        if wq_ref is not None:
            wq_ref[:, cols] = w
        acc_ref[:, cols] += jnp.dot(lhs, w, preferred_element_type=F32)


def _proj_norm_kernel(x_ref, w_ref, res_ref, g_ref, b_ref, of_ref, *rest, nk, emit, bf16_out):
    ob_ref, rest = (rest[0], rest[1:]) if bf16_out else (None, rest)
    wq_ref = rest[0] if emit else None
    if nk == 1:
        w = _bf16(w_ref[...])
        if emit:
            wq_ref[...] = w
        sub = jnp.dot(x_ref[...], w, preferred_element_type=F32)
        _post_norm_store(res_ref[...], sub, g_ref, b_ref, of_ref, ob_ref)
        return
    acc_ref = rest[-1]
    k = pl.program_id(1)
    _zero_on_first(acc_ref, k)
    _accumulate_dot(acc_ref, x_ref[...], w_ref, wq_ref)

    @pl.when(k == nk - 1)
    def _():
        _post_norm_store(res_ref[...], acc_ref[...], g_ref, b_ref, of_ref, ob_ref)


def _proj_norm(x, w, res, g, b, *, emit=False, bf16_out=True, tm=512):
    m, kdim = x.shape
    n = w.ncols
    tm = _row_tile(m, tm)
    assert not emit or m == tm
    tk = kdim if w.is_bf16 else min(kdim, 1024)
    nk = kdim // tk
    row = lambda i, k: (i, 0)
    emit_specs = [pl.BlockSpec((tk, n), lambda i, k: (k, 0))] if emit else []
    emit_shapes = [jax.ShapeDtypeStruct((kdim, n), BF16)] if emit else []
    w_spec = w.spec((tk, n), lambda i, k: (k, 0))
    if nk == 1:
        w_spec = pl.BlockSpec(w_spec.block_shape, w_spec.index_map, pipeline_mode=pl.Buffered(1))
    return pl.pallas_call(
        functools.partial(_proj_norm_kernel, nk=nk, emit=emit, bf16_out=bf16_out),
        grid=(m // tm, nk),
        in_specs=[pl.BlockSpec((tm, tk), lambda i, k: (i, k)),
                  w_spec,
                  pl.BlockSpec((tm, n), row),
                  pl.BlockSpec((1, n), lambda i, k: (0, 0)),
                  pl.BlockSpec((1, n), lambda i, k: (0, 0))],
        out_specs=[pl.BlockSpec((tm, n), row)] * (2 if bf16_out else 1) + emit_specs,
        out_shape=[jax.ShapeDtypeStruct((m, n), F32)] + ([jax.ShapeDtypeStruct((m, n), BF16)] if bf16_out else [])
        + emit_shapes,
        scratch_shapes=[pltpu.VMEM((tm, n), F32)] if nk > 1 else [],
        compiler_params=_params(("parallel", "arbitrary")),
        name="proj_norm",
    )(x, w.array, res, g, b)


def _mlp_kernel(x_ref, w1_ref, w2_ref, g_ref, b_ref, of_ref, *rest, nf, emit, bf16_out):
    ob_ref, rest = (rest[0], rest[1:]) if bf16_out else (None, rest)
    xb_ref, acc_ref = rest[-2:]
    f = pl.program_id(1)

    @pl.when(f == 0)
    def _():
        acc_ref[...] = jnp.zeros_like(acc_ref)
        xb_ref[...] = x_ref[...].astype(BF16)

    w1 = _bf16(w1_ref[...])
    if emit:
        rest[0][...] = w1
    h = jnp.dot(xb_ref[...], w1, preferred_element_type=F32)
    h = jnp.square(jnp.maximum(h, 0.0)).astype(BF16)
    _accumulate_dot(acc_ref, h, w2_ref, rest[1] if emit else None)

    @pl.when(f == nf - 1)
    def _():
        _post_norm_store(x_ref[...], acc_ref[...], g_ref, b_ref, of_ref, ob_ref)


def _mlp(x, w1, w2, g, b, *, emit=False, bf16_out=True, tm=512):
    m, d = x.shape
    dff = w1.ncols
    tm = _row_tile(m, tm)
    assert not emit or m == tm
    tf = 1024 if w1.is_bf16 else 512
    nf = dff // tf
    row = lambda i, f: (i, 0)
    emit_specs = [pl.BlockSpec((d, tf), lambda i, f: (0, f)), pl.BlockSpec((tf, d), lambda i, f: (f, 0))] if emit else []
    emit_shapes = [jax.ShapeDtypeStruct((d, dff), BF16), jax.ShapeDtypeStruct((dff, d), BF16)] if emit else []
    return pl.pallas_call(
        functools.partial(_mlp_kernel, nf=nf, emit=emit, bf16_out=bf16_out),
        grid=(m // tm, nf),
        in_specs=[pl.BlockSpec((tm, d), row),
                  w1.spec((d, tf), lambda i, f: (0, f)),
                  w2.spec((tf, d), lambda i, f: (f, 0)),
                  pl.BlockSpec((1, d), lambda i, f: (0, 0)),
                  pl.BlockSpec((1, d), lambda i, f: (0, 0))],
        out_specs=[pl.BlockSpec((tm, d), row)] * (2 if bf16_out else 1) + emit_specs,
        out_shape=[jax.ShapeDtypeStruct((m, d), F32)] + ([jax.ShapeDtypeStruct((m, d), BF16)] if bf16_out else [])
        + emit_shapes,
        scratch_shapes=[pltpu.VMEM((tm, d), BF16), pltpu.VMEM((tm, d), F32)],
        compiler_params=_params(("parallel", "arbitrary")),
        name="mlp",
    )(x, w1.array, w2.array, g, b)


RET_HEADS_PER_STEP = 8


def _rotary(x, cos, sin):
    half = x.shape[-1] // 2
    x1, x2 = x[:, :half], x[:, half:]
    return jnp.concatenate([x1 * cos - x2 * sin, x1 * sin + x2 * cos], axis=-1)


def _retention_kernel(*refs, n_chunks, has_state0, q_scale, hb, dk_dim, dv_dim):
    if has_state0:
        s0_ref, refs = refs[0], refs[1:]
    (q_ref, k_ref, v_ref, g_ref, cos_ref, sin_ref, din_ref, dq_ref, dk_ref, dc_ref,
     o_ref, sout_ref, s_ref) = refs
    c = pl.program_id(2)

    @pl.when(c == 0)
    def _():
        if has_state0:
            s_ref[...] = s0_ref[0]
        else:
            s_ref[...] = jnp.zeros_like(s_ref)

    cos, sin = cos_ref[...], sin_ref[...]
    for i in range(hb):
        kcols = slice(i * dk_dim, (i + 1) * dk_dim)
        vcols = slice(i * dv_dim, (i + 1) * dv_dim)
        q = (_rotary(q_ref[:, kcols], cos, sin) * q_scale).astype(BF16)
        k = _rotary(k_ref[:, kcols], cos, sin)
        v = v_ref[:, vcols]
        state = s_ref[i]

        scores = lax.dot_general(q, k.astype(BF16), (((1,), (1,)), ((), ())),
                                 preferred_element_type=F32) * din_ref[i]
        inner = jnp.dot(scores.astype(BF16), v, preferred_element_type=F32)
        cross = jnp.dot(q, state.astype(BF16), preferred_element_type=F32) * dq_ref[i]
        kd = (k * dk_ref[i]).astype(BF16)
        s_ref[i] = dc_ref[i] * state + lax.dot_general(kd, v, (((0,), (0,)), ((), ())),
                                                       preferred_element_type=F32)

        o = inner + cross
        mu = jnp.mean(o, axis=-1, keepdims=True)
        oc = o - mu
        var = jnp.mean(oc * oc, axis=-1, keepdims=True)
        gate = g_ref[:, vcols]
        gate = gate / (1.0 + jnp.exp(-gate))
        o_ref[:, vcols] = (gate * (oc * lax.rsqrt(var + GN_EPS))).astype(o_ref.dtype)

    @pl.when(c == n_chunks - 1)
    def _():
        sout_ref[0] = s_ref[...]


def _retention_tables(chunk, n_valid):
    h = RET_HEADS
    log_g = jnp.log1p(-jnp.power(2.0, -5.0 - jnp.arange(h, dtype=F32)))
    idx = jnp.arange(chunk, dtype=F32)
    diff = idx[:, None] - idx[None, :]
    din = jnp.where(diff[None] >= 0, jnp.exp(log_g[:, None, None] * jnp.maximum(diff, 0.0)[None]), 0.0)
    dq = jnp.exp(log_g[:, None] * (idx + 1.0)[None, :])[..., None]
    dk = jnp.exp(log_g[:, None] * (n_valid - 1.0 - idx)[None, :])[..., None]
    dc = jnp.exp(log_g * n_valid).reshape(h, 1, 1)
    return din, dq, dk, dc


def _rotary_tables(pos, half):
    inv = ROPE_BASE ** (-jnp.arange(half, dtype=F32) / half)
    ang = pos.astype(F32)[:, None] * inv[None, :]
    return jnp.cos(ang), jnp.sin(ang)


def _retention(qkg, v, state0, *, batch, seq, chunk, n_valid, pos0):
    h = RET_HEADS
    hb = RET_HEADS_PER_STEP
    ng = h // hb
    dv_dim = v.shape[1] // h
    dk_dim = dv_dim // 2
    assert qkg.shape[1] == 2 * h * dk_dim + h * dv_dim
    n_chunks = seq // chunk
    cos, sin = _rotary_tables(pos0 + jnp.arange(seq, dtype=jnp.int32), dk_dim // 2)
    din, dq, dk, dc = _retention_tables(chunk, n_valid)
    has_state0 = state0 is not None

    row = lambda b, hg, c: b * n_chunks + c
    per_group = lambda b, hg, c: (hg, 0, 0)
    in_specs = [
        pl.BlockSpec((chunk, hb * dk_dim), lambda b, hg, c: (row(b, hg, c), hg)),
        pl.BlockSpec((chunk, hb * dk_dim), lambda b, hg, c: (row(b, hg, c), ng + hg)),
        pl.BlockSpec((chunk, hb * dv_dim), lambda b, hg, c: (row(b, hg, c), hg)),
        pl.BlockSpec((chunk, hb * dv_dim), lambda b, hg, c: (row(b, hg, c), ng + hg)),
        pl.BlockSpec((chunk, dk_dim // 2), lambda b, hg, c: (c, 0)),
        pl.BlockSpec((chunk, dk_dim // 2), lambda b, hg, c: (c, 0)),
        pl.BlockSpec((hb, chunk, chunk), per_group),
        pl.BlockSpec((hb, chunk, 1), per_group),
        pl.BlockSpec((hb, chunk, 1), per_group),
        pl.BlockSpec((hb, 1, 1), per_group),
    ]
    args = [qkg, qkg, v, qkg, cos, sin, din, dq, dk, dc]
    state_spec = pl.BlockSpec((1, hb, dk_dim, dv_dim), lambda b, hg, c: (b, hg, 0, 0))
    if has_state0:
        in_specs.insert(0, state_spec)
        args.insert(0, state0)
    o, s_out = pl.pallas_call(
        functools.partial(_retention_kernel, n_chunks=n_chunks, has_state0=has_state0,
                          q_scale=dk_dim ** -0.5, hb=hb, dk_dim=dk_dim, dv_dim=dv_dim),
        grid=(batch, ng, n_chunks),
        in_specs=in_specs,
        out_specs=[pl.BlockSpec((chunk, hb * dv_dim), lambda b, hg, c: (row(b, hg, c), hg)), state_spec],
        out_shape=[jax.ShapeDtypeStruct((batch * seq, h * dv_dim), BF16),
                   jax.ShapeDtypeStruct((batch, h, dk_dim, dv_dim), F32)],
        scratch_shapes=[pltpu.VMEM((hb, dk_dim, dv_dim), F32)],
        compiler_params=_params(("parallel", "parallel", "arbitrary")),
        name="retention",
    )(*args)
    return o, s_out


def _softplus2(z2):
    return jnp.maximum(z2, 0.0) + jnp.log2(1.0 + jnp.exp2(-jnp.abs(z2)))


_NT = (((1,), (1,)), ((), ()))


def _sb_prompt_kernel(q_ref, k_ref, v_ref, bias_ref, o_ref, *, tq, tk):
    qi = pl.program_id(2)
    dh = q_ref.shape[-1]
    nsub = tq // tk
    q = q_ref[...]
    bias2 = bias_ref[0] * LOG2_E

    r = lax.broadcasted_iota(jnp.int32, (tk, tk), 0)
    c = lax.broadcasted_iota(jnp.int32, (tk, tk), 1)
    suffix = jnp.where(r > c, 1.0, 0.0).astype(BF16)
    causal = c < r

    def run(groups):
        raws = [[lax.dot_general(qr, k_ref[pl.ds(st, n * tk), :], _NT, preferred_element_type=F32)
                 for st, n in segs] for qr, segs, _, _, _ in groups]
        tiles = []
        for raw_segs, (_, segs, diagonal, _, _) in zip(raws, groups):
            group_tiles = []
            for raw, (_, n) in zip(raw_segs, segs):
                for s in reversed(range(n)):
                    z2 = raw[:, s * tk:(s + 1) * tk] + bias2
                    sp2 = _softplus2(z2)
                    masked = diagonal and not group_tiles
                    if masked:
                        sp2 = jnp.where(causal, sp2, 0.0)
                    group_tiles.append((z2 - sp2, sp2.astype(BF16), jnp.sum(sp2, axis=-1, keepdims=True), masked))
            tiles.append(group_tiles)
        sums = [[jnp.dot(t[1], suffix, preferred_element_type=F32) for t in group_tiles] for group_tiles in tiles]
        out = []
        for group_tiles, group_sums, (_, segs, _, carry, acc) in zip(tiles, sums, groups):
            ws = []
            for (logit, _, total, masked), in_tile in zip(group_tiles, group_sums):
                w = jnp.exp2(logit - (in_tile + carry))
                if masked:
                    w = jnp.where(causal, w, 0.0)
                ws.append(w.astype(BF16))
                carry = carry + total
            out.append((carry, ws, acc, segs))
        results = []
        for carry, ws, acc, segs in out:
            done = 0
            for st, n in segs:
                w_seg = jnp.concatenate(ws[done:done + n][::-1], axis=-1)
                v_seg = v_ref[pl.ds(st, n * tk), :]
                done += n
                acc = acc + jnp.dot(w_seg, v_seg, preferred_element_type=F32)
            results.append((carry, acc))
        return results

    q_groups = [q[g * tk:(g + 1) * tk] for g in range(nsub)]
    chunk_start = lambda j: pl.multiple_of(j * tq, tq)
    state = tuple(run([(q_groups[g], [(chunk_start(qi), g + 1)], True,
                        jnp.zeros((tk, 1), F32), jnp.zeros((tk, dh), F32)) for g in range(nsub)]))

    def chunks(js, st):
        segs = [(chunk_start(j), nsub) for j in js]
        return tuple(run([(q_groups[g], segs, False, st[g][0], st[g][1]) for g in range(nsub)]))

    odd = qi % 2
    state = lax.fori_loop(0, odd, lambda it, st: chunks([qi - 1], st), state)
    first = qi - 1 - odd
    state = lax.fori_loop(0, qi // 2, lambda it, st: chunks([first - 2 * it, first - 2 * it - 1], st), state)
    for g in range(nsub):
        o_ref[g * tk:(g + 1) * tk, :] = state[g][1].astype(o_ref.dtype)


def _sb_prompt(q, k, v, bias, *, batch, seq, tq=512, tk=256):
    h = SB_HEADS
    dh = q.shape[1] // h
    nq = seq // tq
    return pl.pallas_call(
        functools.partial(_sb_prompt_kernel, tq=tq, tk=tk),
        grid=(batch, h, nq),
        in_specs=[pl.BlockSpec((tq, dh), lambda b, hh, i: (b * nq + i, hh)),
                  pl.BlockSpec((seq, dh), lambda b, hh, i: (b, hh)),
                  pl.BlockSpec((seq, dh), lambda b, hh, i: (b, hh)),
                  pl.BlockSpec((1, 1, 1), lambda b, hh, i: (hh, 0, 0))],
        out_specs=pl.BlockSpec((tq, dh), lambda b, hh, i: (b * nq + i, hh)),
        out_shape=jax.ShapeDtypeStruct(q.shape, BF16),
        compiler_params=_params(("parallel", "parallel", "parallel")),
        name="sb_prompt",
    )(q, k, v, bias.reshape(h, 1, 1))


def _sb_decode_kernel(pt_ref, q_ref, bias_ref, kn_ref, vn_ref, *refs, pages_per_step, n_steps, page, heads,
                      t_new):
    del pt_ref
    k_refs = refs[:pages_per_step]
    v_refs = refs[pages_per_step:2 * pages_per_step]
    o_ref, acc_ref, carry_ref = refs[2 * pages_per_step:]
    p = pl.program_id(1)
    hh = SUBLANES_F32
    n_groups = heads // hh
    rows = hh * t_new
    lanes = page * hh
    n_tiles = lanes // LANES
    dh = q_ref.shape[-1]

    rr = lax.broadcasted_iota(jnp.int32, (LANES, 2 * LANES), 0)
    cc = lax.broadcasted_iota(jnp.int32, (LANES, 2 * LANES), 1)
    suffix = jnp.where((cc >= LANES) | (rr // hh > cc // hh), 1.0, 0.0).astype(BF16)
    lane = lax.broadcasted_iota(jnp.int32, (rows, lanes), 1)
    row = lax.broadcasted_iota(jnp.int32, (rows, lanes), 0)
    own = lane % hh == row // t_new

    def rows_of(ref, grp):
        return ref[0, :, pl.ds(grp * hh, hh), :].reshape(lanes, dh).astype(BF16)

    def update(page_refs, masked):
        keep = own
        if masked:
            keep = own & (lane // hh < row % t_new)
        work = [(k_ref, v_ref, grp) for grp in range(n_groups) for k_ref, v_ref in page_refs]
        z2s = [lax.dot_general(q_ref[0, grp], rows_of(k_ref, grp), _NT, preferred_element_type=F32)
               + bias_ref[grp] * LOG2_E for k_ref, _, grp in work]
        sp2s = [jnp.where(keep, _softplus2(z2), 0.0) for z2 in z2s]
        sums = []
        for sp2 in sp2s:
            stacked = jnp.concatenate([sp2[:, j * LANES:(j + 1) * LANES] for j in range(n_tiles)], axis=0)
            sums.append(jnp.dot(stacked.astype(BF16), suffix, preferred_element_type=F32))
        ws = []
        run = None
        for i, (_, _, grp) in enumerate(work):
            if i % len(page_refs) == 0:
                run = carry_ref[grp]
            later = [None] * n_tiles
            for j in reversed(range(n_tiles)):
                tile_sums = sums[i][j * rows:(j + 1) * rows]
                later[j] = tile_sums[:, :LANES] + run
                run = run + tile_sums[:, LANES:]
            if (i + 1) % len(page_refs) == 0:
                carry_ref[grp] = run
            w = jnp.exp2(z2s[i] - sp2s[i] - jnp.concatenate(later, axis=-1))
            ws.append(jnp.where(keep, w, 0.0).astype(BF16))
        for grp in range(n_groups):
            acc = acc_ref[grp]
            for i, (_, v_ref, g2) in enumerate(work):
                if g2 == grp:
                    acc = acc + jnp.dot(ws[i], rows_of(v_ref, grp), preferred_element_type=F32)
            acc_ref[grp] = acc

    @pl.when(p == 0)
    def _():
        acc_ref[...] = jnp.zeros_like(acc_ref)
        carry_ref[...] = jnp.zeros_like(carry_ref)
        update([(kn_ref, vn_ref)], True)

    update(list(zip(k_refs, v_refs)), False)

    @pl.when(p == n_steps - 1)
    def _():
        o_ref[0] = acc_ref[...]


def _sb_decode(q, k_new, v_new, cache_k, cache_v, page_table, bias, *, batch, t_new, pages_per_step=8):
    h = SB_HEADS
    hh = SUBLANES_F32
    n_pool, page, _, dh = cache_k.shape
    assert h % hh == 0 and (page * hh) % LANES == 0 and LANES % hh == 0
    n_groups = h // hh
    rows = hh * t_new
    n_pages = page_table.shape[1]
    n_steps = n_pages // pages_per_step
    q_rows = jnp.transpose(q.reshape(batch, t_new, n_groups, hh, dh), (0, 2, 3, 1, 4))
    q_rows = q_rows.reshape(batch, n_groups, rows, dh).astype(BF16)
    bias_rows = jnp.repeat(bias.astype(F32).reshape(n_groups, hh), t_new, axis=1)[..., None]
    pad = lambda a: jnp.pad(a.reshape(batch, t_new, h, dh), ((0, 0), (0, page - t_new), (0, 0), (0, 0)))
    kn, vn = pad(k_new), pad(v_new)

    def page_map(g):
        return lambda b, p, pt: (pt[b, n_pages - 1 - (p * pages_per_step + g)], 0, 0, 0)

    page_specs = [pl.BlockSpec((1, page, h, dh), page_map(g)) for g in range(pages_per_step)]
    per_batch = lambda b, p, pt: (b, 0, 0, 0)
    acc = pl.pallas_call(
        functools.partial(_sb_decode_kernel, pages_per_step=pages_per_step, n_steps=n_steps, page=page, heads=h,
                          t_new=t_new),
        grid_spec=pltpu.PrefetchScalarGridSpec(
            num_scalar_prefetch=1,
            grid=(batch, n_steps),
            in_specs=[pl.BlockSpec((1, n_groups, rows, dh), per_batch),
                      pl.BlockSpec((n_groups, rows, 1), lambda b, p, pt: (0, 0, 0)),
                      pl.BlockSpec((1, page, h, dh), per_batch),
                      pl.BlockSpec((1, page, h, dh), per_batch)] + page_specs + page_specs,
            out_specs=pl.BlockSpec((1, n_groups, rows, dh), per_batch),
            scratch_shapes=[pltpu.VMEM((n_groups, rows, dh), F32), pltpu.VMEM((n_groups, rows, LANES), F32)],
        ),
        out_shape=jax.ShapeDtypeStruct((batch, n_groups, rows, dh), F32),
        compiler_params=_params(("parallel", "arbitrary")),
        name="sb_decode",
    )(page_table, q_rows, bias_rows, kn, vn, *([cache_k] * pages_per_step), *([cache_v] * pages_per_step))
    o = jnp.transpose(acc.reshape(batch, n_groups, hh, t_new, dh), (0, 3, 1, 2, 4))
    return o.reshape(batch * t_new, h * dh).astype(BF16)


def _trunk(x, w, norms, sb_bias, *, batch, seq, pos0, state0, decode, emit, after=None):
    ln_g, ln_b = norms
    m, d = x.shape
    g = lambda layer, i: ln_g[layer, i][None, :]
    b = lambda layer, i: ln_b[layer, i][None, :]
    copies = {}

    def keep(name, outs):
        if emit:
            copies[name] = _Weight(outs[-1])
            return outs[:-1]
        return outs

    hk = w["ret_in"].ncols // 6
    qkg, v = keep("ret_in", _proj(x, w["ret_in"], None, emit=emit, bf16_cols=(2 * hk, 4 * hk)))
    if seq % RET_CHUNK == 0:
        o, s_new = _retention(qkg, v, state0, batch=batch, seq=seq, chunk=RET_CHUNK, n_valid=RET_CHUNK, pos0=pos0)
    else:
        chunk = 2 * SUBLANES_F32
        pad = lambda a: jnp.pad(a.reshape(batch, seq, -1), ((0, 0), (0, chunk - seq), (0, 0))).reshape(batch * chunk, -1)
        o, s_new = _retention(pad(qkg), pad(v), state0, batch=batch, seq=chunk, chunk=chunk,
                              n_valid=seq, pos0=pos0)
        o = o.reshape(batch, chunk, -1)[:, :seq].reshape(m, -1)
    x, = keep("ret_out", _proj_norm(o, w["ret_out"], x, g(0, 0), b(0, 0), emit=emit, bf16_out=False))
    outs = _mlp(x, w["ff1_0"], w["ff2_0"], g(0, 1), b(0, 1), emit=emit)
    if emit:
        copies["ff1_0"], copies["ff2_0"] = _Weight(outs[2]), _Weight(outs[3])
    x, xb = outs[:2]

    q_scale = LOG2_E * (d // SB_HEADS) ** -0.5
    if decode is None:
        k_new, kb, v_new, vb, q = _proj_kvq(xb, w["k"], w["v"], w["sb_q"], q_scale, after)
        o = _sb_prompt(q, kb, vb, sb_bias, batch=batch, seq=seq)
    else:
        cache_k, cache_v, page_table = decode
        k_new, = keep("k", _proj(xb, w["k"], [F32], emit=emit))
        v_new, = keep("v", _proj(xb, w["v"], [F32], emit=emit))
        q, = keep("sb_q", _proj(xb, w["sb_q"], [F32], emit=emit, scale=q_scale))
        o = _sb_decode(q, k_new, v_new, cache_k, cache_v, page_table, sb_bias, batch=batch, t_new=seq)
    x, = keep("sb_o", _proj_norm(o, w["sb_o"], x, g(1, 0), b(1, 0), emit=emit, bf16_out=False))
    outs = _mlp(x, w["ff1_1"], w["ff2_1"], g(1, 1), b(1, 1), emit=emit, bf16_out=False)
    if emit:
        copies["ff1_1"], copies["ff2_1"] = _Weight(outs[1]), _Weight(outs[2])
    return outs[0], s_new, k_new, v_new, copies


def kernel(x_prompt, x_sample, state_ret, cache_k, cache_v, page_table, w_ret_in, w_ret_out, w_kv, w_sb_q,
           w_sb_o, sb_bias, w_ff1, w_ff2, ln_g, ln_b):
    bp, tp, d = x_prompt.shape
    bs, ts, _ = x_sample.shape
    dh = d // SB_HEADS
    n_pages = page_table.shape[1]
    page = cache_k.shape[1]
    weights = {"ret_in": _Weight(w_ret_in, 0), "ret_out": _Weight(w_ret_out, 0),
               "k": _Weight(w_kv, col0=0, ncols=d), "v": _Weight(w_kv, col0=d, ncols=d),
               "sb_q": _Weight(w_sb_q, 0), "sb_o": _Weight(w_sb_o, 0),
               "ff1_0": _Weight(w_ff1, 0), "ff2_0": _Weight(w_ff2, 0),
               "ff1_1": _Weight(w_ff1, 1), "ff2_1": _Weight(w_ff2, 1)}
    norms = (ln_g.astype(F32), ln_b.astype(F32))

    y_s, s_s, k_s, v_s, weights_bf16 = _trunk(
        x_sample.reshape(bs * ts, d), weights, norms, sb_bias[0], batch=bs, seq=ts, pos0=n_pages * page,
        state0=state_ret[0].astype(F32), decode=(cache_k, cache_v, page_table), emit=True)
    y_p, s_p, k_p, v_p, _ = _trunk(
        x_prompt.reshape(bp * tp, d), weights_bf16, norms, sb_bias[0], batch=bp, seq=tp, pos0=0,
        state0=None, decode=None, emit=False, after=y_s[:1, :1])
    heads = lambda a, bb, tt: a.reshape(bb, tt, SB_HEADS, dh)
    return (y_p.reshape(bp, tp, d), y_s.reshape(bs, ts, d),
            s_p[None].astype(state_ret.dtype), heads(k_p, bp, tp), heads(v_p, bp, tp),
            s_s[None].astype(state_ret.dtype), heads(k_s, bs, ts), heads(v_s, bs, ts))
```
